```python
import jax
import jax.numpy as jnp
from jax import lax
import numpy as np


D_MODEL = 2048
BATCH = 8
SEQ = 2048
DEPTH = 2

RMS_EPS = 1e-6
PLE_DIM = 256
SB_HEADS = 8
SB_HEAD_DIM = 128
SB_BLOCK = 128
SB_WIDTH = SB_HEADS * SB_HEAD_DIM
HG_HEADS = 8
HG_KEY_DIM = 128
HG_VAL_DIM = 128
HG_CHUNK = 64
HG_KEY_WIDTH = HG_HEADS * HG_KEY_DIM
HG_VAL_WIDTH = HG_HEADS * HG_VAL_DIM
EVEN_SPLITS = [SB_WIDTH, SB_WIDTH, SB_WIDTH, HG_KEY_WIDTH, HG_KEY_WIDTH, HG_VAL_WIDTH, HG_VAL_WIDTH]
EVEN_IN_WIDTH = sum(EVEN_SPLITS)
EVEN_OUT_WIDTH = SB_WIDTH + HG_VAL_WIDTH
LRU_WIDTH = ((4 * D_MODEL // 3 + 255) // 256) * 256
RG_BLOCK_WIDTH = 256
RG_BLOCKS = LRU_WIDTH // RG_BLOCK_WIDTH
CONV_WIDTH = 4
RG_C = 8.0
D_FF = ((8 * D_MODEL // 3 + 255) // 256) * 256
N_EVEN = (DEPTH + 1) // 2
N_ODD = DEPTH // 2

kernel_name = "hybrid_stickbreak_hgrn2_rglru_block"


def rmsnorm(x, g):
    xf = x.astype(jnp.float32)
    y = xf * lax.rsqrt(jnp.mean(xf * xf, axis=-1, keepdims=True) + RMS_EPS)
    return (y * g.astype(jnp.float32)).astype(x.dtype)


def split_heads(a, n_heads, head_dim):
    b, s, _ = a.shape
    return a.reshape(b, s, n_heads, head_dim).transpose(0, 2, 1, 3)


def merge_heads(a):
    b, h, s, d = a.shape
    return a.transpose(0, 2, 1, 3).reshape(b, s, h * d)


def stick_breaking_attention(q, k, v):
    seq = q.shape[2]
    qf = q.astype(jnp.float32) * (SB_HEAD_DIM ** -0.5)
    kf = k.astype(jnp.float32)
    vf = v.astype(jnp.float32)
    outs = []
    for blk in range(seq // SB_BLOCK):
        q0 = blk * SB_BLOCK
        q1 = q0 + SB_BLOCK
        z = jnp.einsum('bhtd,bhsd->bhts', qf[:, :, q0:q1], kf[:, :, :q1])
        mask = jnp.arange(q1)[None, :] < (q0 + jnp.arange(SB_BLOCK))[:, None]
        log_beta = jax.nn.log_sigmoid(z)
        log_one_minus = jnp.where(mask, log_beta - z, 0.0)
        log_remain = lax.cumsum(log_one_minus, axis=3, reverse=True) - log_one_minus
        w = jnp.where(mask, jnp.exp(log_beta + log_remain), 0.0)
        outs.append(jnp.einsum('bhts,bhsd->bhtd', w, vf[:, :, :q1]))
    return jnp.concatenate(outs, axis=2).astype(q.dtype)


def hgrn2_chunkwise(q, k, v, log_f):
    b, h, s, dk = q.shape
    dv = v.shape[-1]
    n_chunks = s // HG_CHUNK

    def to_chunks(a):
        return a.reshape(b, h, n_chunks, HG_CHUNK, a.shape[-1]).transpose(2, 0, 1, 3, 4)

    causal = (jnp.arange(HG_CHUNK)[:, None] >= jnp.arange(HG_CHUNK)[None, :])[:, :, None]

    def step(state, inp):
        qi, ki, vi, gi = inp
        cum = jnp.cumsum(gi, axis=2)
        o_inter = jnp.einsum('bhtk,bhkv->bhtv', qi * jnp.exp(cum), state)
        diff = cum[:, :, :, None, :] - cum[:, :, None, :, :]
        decay = jnp.exp(jnp.where(causal, diff, -jnp.inf))
        scores = jnp.einsum('bhtk,bhsk,bhtsk->bhts', qi, ki, decay)
        o_intra = jnp.einsum('bhts,bhsv->bhtv', scores, vi)
        last = cum[:, :, -1:, :]
        k_dec = ki * jnp.exp(last - cum)
        new_state = state * jnp.exp(last[:, :, 0, :, None]) + jnp.einsum('bhsk,bhsv->bhkv', k_dec, vi)
        return new_state, o_inter + o_intra

    state0 = jnp.zeros((b, h, dk, dv), jnp.float32)
    _, o = lax.scan(step, state0, (to_chunks(q), to_chunks(k), to_chunks(v), to_chunks(log_f)))
    return o.transpose(1, 2, 0, 3, 4).reshape(b, h, s, dv)


def hgrn2(hq, hf, hi, hg, lb, norm_g):
    lbf = lb.astype(jnp.float32)
    f = lbf + (1.0 - lbf) * jax.nn.sigmoid(hf.astype(jnp.float32))
    q = jax.nn.silu(hq.astype(jnp.float32))
    o = hgrn2_chunkwise(split_heads(q, HG_HEADS, HG_KEY_DIM),
                        split_heads(1.0 - f, HG_HEADS, HG_KEY_DIM),
                        split_heads(hi.astype(jnp.float32), HG_HEADS, HG_VAL_DIM),
                        split_heads(jnp.log(f), HG_HEADS, HG_KEY_DIM))
    o = rmsnorm(o, norm_g)
    return (merge_heads(o) * jax.nn.silu(hg.astype(jnp.float32))).astype(hq.dtype)


def even_mixer(h, w_in, w_out, lb, hg_norm_g):
    sq, sk, sv, hq, hf, hi, hg = jnp.split(h @ w_in, np.cumsum(EVEN_SPLITS)[:-1].tolist(), axis=-1)
    a_out = merge_heads(stick_breaking_attention(split_heads(sq, SB_HEADS, SB_HEAD_DIM),
                                                 split_heads(sk, SB_HEADS, SB_HEAD_DIM),
                                                 split_heads(sv, SB_HEADS, SB_HEAD_DIM)))
    b_out = hgrn2(hq, hf, hi, hg, lb, hg_norm_g)
    return jnp.concatenate([a_out.astype(h.dtype), b_out.astype(h.dtype)], axis=-1) @ w_out


def causal_depthwise_conv(x, w, b):
    s = x.shape[1]
    xp = jnp.pad(x, ((0, 0), (CONV_WIDTH - 1, 0), (0, 0)))
    y = b
    for tap in range(CONV_WIDTH):
        y = y + xp[:, CONV_WIDTH - 1 - tap:CONV_WIDTH - 1 - tap + s] * w[tap]
    return y


def block_diag_linear(x, w, b):
    bsz, s, _ = x.shape
    xb = x.reshape(bsz, s, RG_BLOCKS, RG_BLOCK_WIDTH)
    return (jnp.einsum('bsni,nio->bsno', xb, w) + b).reshape(bsz, s, LRU_WIDTH)


def rg_lru(x, wa, ba, wx, bx, lam):
    s = x.shape[1]
    r = jax.nn.sigmoid(block_diag_linear(x, wa, ba).astype(jnp.float32))
    i = jax.nn.sigmoid(block_diag_linear(x, wx, bx).astype(jnp.float32))
    log_a = -RG_C * r * jax.nn.softplus(-lam.astype(jnp.float32))
    a = jnp.exp(log_a)
    mult = jnp.sqrt(-jnp.expm1(2.0 * log_a))
    mult = jnp.where((jnp.arange(s) == 0)[None, :, None], 1.0, mult)
    u = x.astype(jnp.float32) * i * mult

    def combine(left, right):
        a1, b1 = left
        a2, b2 = right
        return a1 * a2, a2 * b1 + b2

    _, hs = lax.associative_scan(combine, (a, u), axis=1)
    return hs.astype(x.dtype)


def odd_mixer(h, w_in, conv_w, conv_b, wa, ba, wx, bx, lam, w_out):
    gate_branch, x_branch = jnp.split(h @ w_in, 2, axis=-1)
    y = rg_lru(causal_depthwise_conv(x_branch, conv_w, conv_b), wa, ba, wx, bx, lam)
    return (jax.nn.gelu(gate_branch) * y) @ w_out


def swiglu(h, w_gate_up, w_down):
    g, u = jnp.split(h @ w_gate_up, 2, axis=-1)
    return (jax.nn.silu(g) * u) @ w_down


def per_layer_embedding(h, p_i, w_up, w_gate, g):
    e = p_i @ w_up
    gate = jax.nn.sigmoid(h @ w_gate)
    return rmsnorm(gate * e, g)


def setup_inputs(seed: int = 0) -> dict:
    key = jax.random.key(seed)
    ks = jax.random.split(key, 24)

    def nrm(i, shape, scale):
        return scale * jax.random.normal(ks[i], shape, jnp.float32)

    u = jax.random.uniform(ks[18], (N_ODD, LRU_WIDTH), jnp.float32, 0.9, 0.999)
    a0 = u ** (1.0 / RG_C)
    rg_lambda = jnp.log(a0) - jnp.log1p(-a0)
    return {
        'x': nrm(0, (BATCH, SEQ, D_MODEL), 1.0),
        'p': nrm(1, (DEPTH, BATCH, SEQ, PLE_DIM), 1.0),
        'mix_pre_g': 1.0 + nrm(2, (DEPTH, D_MODEL), 0.05),
        'mix_post_g': 1.0 + nrm(3, (DEPTH, D_MODEL), 0.05),
        'ffn_pre_g': 1.0 + nrm(4, (DEPTH, D_MODEL), 0.05),
        'ffn_post_g': 1.0 + nrm(5, (DEPTH, D_MODEL), 0.05),
        'ple_norm_g': 1.0 + nrm(6, (DEPTH, D_MODEL), 0.05),
        'w_in_even': nrm(7, (N_EVEN, D_MODEL, EVEN_IN_WIDTH), D_MODEL ** -0.5),
        'w_out_even': nrm(8, (N_EVEN, EVEN_OUT_WIDTH, D_MODEL), EVEN_OUT_WIDTH ** -0.5),
        'hg_lb_logits': nrm(9, (N_EVEN + 1, HG_KEY_WIDTH), 0.5),
        'hg_norm_g': 1.0 + nrm(10, (N_EVEN, HG_VAL_DIM), 0.05),
        'w_in_odd': nrm(11, (N_ODD, D_MODEL, 2 * LRU_WIDTH), D_MODEL ** -0.5),
        'conv_w': nrm(12, (N_ODD, CONV_WIDTH, LRU_WIDTH), CONV_WIDTH ** -0.5),
        'conv_b': nrm(13, (N_ODD, LRU_WIDTH), 0.02),
        'rg_wa': nrm(14, (N_ODD, RG_BLOCKS, RG_BLOCK_WIDTH, RG_BLOCK_WIDTH), RG_BLOCK_WIDTH ** -0.5),
        'rg_ba': nrm(15, (N_ODD, RG_BLOCKS, RG_BLOCK_WIDTH), 0.02),
        'rg_wx': nrm(16, (N_ODD, RG_BLOCKS, RG_BLOCK_WIDTH, RG_BLOCK_WIDTH), RG_BLOCK_WIDTH ** -0.5),
        'rg_bx': nrm(17, (N_ODD, RG_BLOCKS, RG_BLOCK_WIDTH), 0.02),
        'rg_lambda': rg_lambda,
        'w_out_odd': nrm(19, (N_ODD, LRU_WIDTH, D_MODEL), LRU_WIDTH ** -0.5),
        'w_gate_up': nrm(20, (DEPTH, D_MODEL, 2 * D_FF), D_MODEL ** -0.5),
        'w_down': nrm(21, (DEPTH, D_FF, D_MODEL), D_FF ** -0.5),
        'w_ple_up': nrm(22, (DEPTH, PLE_DIM, D_MODEL), PLE_DIM ** -0.5),
        'w_ple_gate': nrm(23, (DEPTH, D_MODEL, D_MODEL), D_MODEL ** -0.5),
    }


def reference(x, p, mix_pre_g, mix_post_g, ffn_pre_g, ffn_post_g, ple_norm_g,
              w_in_even, w_out_even, hg_lb_logits, hg_norm_g,
              w_in_odd, conv_w, conv_b, rg_wa, rg_ba, rg_wx, rg_bx, rg_lambda, w_out_odd,
              w_gate_up, w_down, w_ple_up, w_ple_gate):
    lb_all = jnp.cumsum(jax.nn.softmax(hg_lb_logits.astype(jnp.float32), axis=0), axis=0)
    h = x
    for i in range(DEPTH):
        j = i // 2
        n = rmsnorm(h, mix_pre_g[i])
        if i % 2 == 0:
            m = even_mixer(n, w_in_even[j], w_out_even[j], lb_all[j], hg_norm_g[j])
        else:
            m = odd_mixer(n, w_in_odd[j], conv_w[j], conv_b[j], rg_wa[j], rg_ba[j],
                          rg_wx[j], rg_bx[j], rg_lambda[j], w_out_odd[j])
        h = h + rmsnorm(m, mix_post_g[i])
        h = h + rmsnorm(swiglu(rmsnorm(h, ffn_pre_g[i]), w_gate_up[i], w_down[i]), ffn_post_g[i])
        h = h + per_layer_embedding(h, p[i], w_ple_up[i], w_ple_gate[i], ple_norm_g[i])
    return h
```

```python
import functools

import numpy as np
import jax
import jax.numpy as jnp
from jax import lax
from jax.experimental import pallas as pl
from jax.experimental.pallas import tpu as pltpu

F32 = jnp.float32
BF16 = jnp.bfloat16

RMS_EPS = 1e-6
HEAD_DIM = 128
RG_BLOCK_WIDTH = 256
CONV_WIDTH = 4
RG_C = 8.0
V7X_VMEM_BYTES = 64 * 1024 * 1024
VMEM_LIMIT_BYTES = V7X_VMEM_BYTES - 8 * 1024 * 1024
SUBLANES = 8
HG_BLOCK = 128
HG_LEVELS = 7


def _tile(n, target, quantum):
    if n <= target:
        return n
    t = (target // quantum) * quantum
    while t > quantum and n % t:
        t -= quantum
    assert n % t == 0, (n, target, quantum)
    return t


def _params(*semantics):
    return pltpu.CompilerParams(dimension_semantics=semantics, vmem_limit_bytes=VMEM_LIMIT_BYTES)


def _sigmoid(x):
    return 1.0 / (1.0 + jnp.exp(-x))


def _rms_scale(x):
    return lax.rsqrt(jnp.mean(x * x, axis=-1, keepdims=True) + RMS_EPS)


def _dot(a, b):
    return jnp.dot(a, b, preferred_element_type=F32)


def _dot_nt(a, b):
    return lax.dot_general(a, b, (((1,), (1,)), ((), ())), preferred_element_type=F32)


def _split_bf16(x):
    hi = x.astype(BF16)
    lo = (x - hi.astype(F32)).astype(BF16)
    return hi, lo


def _norm_matmul_kernel(x_ref, g_ref, w_ref, o_ref, xn_ref):
    @pl.when(pl.program_id(1) == 0)
    def _():
        x = x_ref[...]
        xn_ref[...] = (x * _rms_scale(x) * g_ref[...]).astype(BF16)

    o_ref[...] = _dot(xn_ref[...], w_ref[...]).astype(o_ref.dtype)


def _norm_swiglu_kernel(x_ref, g_ref, wg_ref, wu_ref, o_ref, xn_ref):
    @pl.when(pl.program_id(1) == 0)
    def _():
        x = x_ref[...]
        xn_ref[...] = (x * _rms_scale(x) * g_ref[...]).astype(BF16)

    xn = xn_ref[...]
    gate = _dot(xn, wg_ref[...])
    up = _dot(xn, wu_ref[...])
    o_ref[...] = (gate * _sigmoid(gate) * up).astype(o_ref.dtype)


def _norm_matmul(x, g, w, out_dtype, *, tm_target=1024, tn_target=512):
    t, d = x.shape
    n = w.shape[1]
    tm = _tile(t, tm_target, SUBLANES)
    tn = _tile(n, tn_target, 128)
    return pl.pallas_call(
        _norm_matmul_kernel,
        grid=(t // tm, n // tn),
        in_specs=[
            pl.BlockSpec((tm, d), lambda i, j: (i, 0)),
            pl.BlockSpec((1, d), lambda i, j: (0, 0)),
            pl.BlockSpec((d, tn), lambda i, j: (0, j)),
        ],
        out_specs=pl.BlockSpec((tm, tn), lambda i, j: (i, j)),
        out_shape=jax.ShapeDtypeStruct((t, n), out_dtype),
        scratch_shapes=[pltpu.VMEM((tm, d), BF16)],
        compiler_params=_params("arbitrary", "arbitrary"),
        name="norm_matmul",
    )(x, g.reshape(1, d), w)


def _norm_swiglu(x, g, w_gate_up, *, tm_target=1024, tn_target=512):
    t, d = x.shape
    d_ff = w_gate_up.shape[1] // 2
    tm = _tile(t, tm_target, SUBLANES)
    tn = _tile(d_ff, tn_target, 128)
    nj = d_ff // tn
    return pl.pallas_call(
        _norm_swiglu_kernel,
        grid=(t // tm, nj),
        in_specs=[
            pl.BlockSpec((tm, d), lambda i, j: (i, 0)),
            pl.BlockSpec((1, d), lambda i, j: (0, 0)),
            pl.BlockSpec((d, tn), lambda i, j: (0, j)),
            pl.BlockSpec((d, tn), lambda i, j: (0, j + nj)),
        ],
        out_specs=pl.BlockSpec((tm, tn), lambda i, j: (i, j)),
        out_shape=jax.ShapeDtypeStruct((t, d_ff), BF16),
        scratch_shapes=[pltpu.VMEM((tm, d), BF16)],
        compiler_params=_params("arbitrary", "arbitrary"),
        name="norm_swiglu",
    )(x, g.reshape(1, d), w_gate_up, w_gate_up)


def _matmul_norm_residual_kernel(x_ref, w_ref, g_ref, h_ref, o_ref, acc_ref, *, nk):
    k = pl.program_id(1)
    part = _dot(x_ref[...].astype(BF16), w_ref[...])

    def finish(m):
        o_ref[...] = h_ref[...] + m * _rms_scale(m) * g_ref[...]

    if nk == 1:
        finish(part)
        return

    @pl.when(k == 0)
    def _():
        acc_ref[...] = part

    @pl.when(jnp.logical_and(k > 0, k < nk - 1))
    def _():
        acc_ref[...] += part

    @pl.when(k == nk - 1)
    def _():
        finish(acc_ref[...] + part)


def _matmul_norm_residual(x, w, g, h, *, tm_target=512, tk_target=512):
    t, kdim = x.shape
    d = w.shape[1]
    tm = _tile(t, tm_target, SUBLANES)
    tk = _tile(kdim, tk_target, 128)
    nk = kdim // tk
    return pl.pallas_call(
        functools.partial(_matmul_norm_residual_kernel, nk=nk),
        grid=(t // tm, nk),
        in_specs=[
            pl.BlockSpec((tm, tk), lambda i, k: (i, k)),
            pl.BlockSpec((tk, d), lambda i, k: (k, 0)),
            pl.BlockSpec((1, d), lambda i, k: (0, 0)),
            pl.BlockSpec((tm, d), lambda i, k: (i, 0)),
        ],
        out_specs=pl.BlockSpec((tm, d), lambda i, k: (i, 0)),
        out_shape=jax.ShapeDtypeStruct((t, d), F32),
        scratch_shapes=[pltpu.VMEM((tm, d), F32)],
        compiler_params=_params("arbitrary", "arbitrary"),
        name="matmul_norm_residual",
    )(x, w, g.reshape(1, d), h)


def _ple_kernel(h_ref, p_ref, wup_ref, wg_ref, g_ref, o_ref):
    h = h_ref[...]
    e = _dot(p_ref[...].astype(BF16), wup_ref[...])
    gate = _sigmoid(_dot(h.astype(BF16), wg_ref[...]))
    y = gate * e
    o_ref[...] = h + y * _rms_scale(y) * g_ref[...]


def _per_layer_embedding(h, p, w_up, w_gate, g, *, tm_target=256):
    t, d = h.shape
    pd = p.shape[1]
    tm = _tile(t, tm_target, SUBLANES)
    return pl.pallas_call(
        _ple_kernel,
        grid=(t // tm,),
        in_specs=[
            pl.BlockSpec((tm, d), lambda i: (i, 0)),
            pl.BlockSpec((tm, pd), lambda i: (i, 0)),
            pl.BlockSpec((pd, d), lambda i: (0, 0)),
            pl.BlockSpec((d, d), lambda i: (0, 0)),
            pl.BlockSpec((1, d), lambda i: (0, 0)),
        ],
        out_specs=pl.BlockSpec((tm, d), lambda i: (i, 0)),
        out_shape=jax.ShapeDtypeStruct((t, d), F32),
        compiler_params=_params("arbitrary"),
        name="per_layer_embedding",
    )(h, p, w_up, w_gate, g.reshape(1, d))


def _sb_attention_kernel(q_ref, k_ref, v_ref, u2_ref, o_ref, *, tq, tk, scale):
    i = pl.program_id(2)
    q = (q_ref[0] * scale).astype(BF16)
    u2 = u2_ref[...]
    ratio = tq // tk

    def tile(j, acc, rem, masked):
        start = pl.multiple_of(j * tk, tk)
        kb = k_ref[0, pl.ds(start, tk), :].astype(BF16)
        vb = v_ref[0, pl.ds(start, tk), :].astype(BF16)
        z = _dot_nt(q, kb)
        sp = jnp.maximum(z, 0.0) + jnp.log1p(jnp.exp(-jnp.abs(z)))
        log1m = -sp
        if masked:
            t_idx = i * tq + lax.broadcasted_iota(jnp.int32, (tq, tk), 0)
            s_idx = j * tk + lax.broadcasted_iota(jnp.int32, (tq, tk), 1)
            mask = s_idx < t_idx
            log1m = jnp.where(mask, log1m, 0.0)
        hi, lo = _split_bf16(log1m)
        c = _dot(hi, u2) + _dot(lo, u2)
        logw = (z - sp) + c[:, :tk] + rem
        w = jnp.exp(logw)
        if masked:
            w = jnp.where(mask, w, 0.0)
        acc = acc + _dot(w.astype(BF16), vb)
        rem = rem + c[:, tk:]
        return acc, rem

    acc = jnp.zeros((tq, HEAD_DIM), F32)
    rem = jnp.zeros((tq, tk), F32)
    for jj in reversed(range(ratio)):
        acc, rem = tile(i * ratio + jj, acc, rem, True)

    def body(n, carry):
        return tile(i * ratio - 1 - n, carry[0], carry[1], False)

    acc, rem = lax.fori_loop(0, i * ratio, body, (acc, rem))
    o_ref[0] = acc.astype(o_ref.dtype)


def _sb_attention(proj, n_heads, *, tq_target=256, tk=128):
    b, s, _ = proj.shape
    tq = _tile(s, tq_target, tk)
    tk = min(tk, tq)
    rows = np.arange(tk)
    u2 = np.concatenate([(rows[:, None] > rows[None, :]), np.ones((tk, tk), bool)], axis=1)
    u2 = jnp.asarray(u2, BF16)
    kern = functools.partial(_sb_attention_kernel, tq=tq, tk=tk, scale=HEAD_DIM ** -0.5)
    return pl.pallas_call(
        kern,
        grid=(b, n_heads, s // tq),
        in_specs=[
            pl.BlockSpec((1, tq, HEAD_DIM), lambda bi, hi, qi: (bi, qi, hi)),
            pl.BlockSpec((1, s, HEAD_DIM), lambda bi, hi, qi: (bi, 0, n_heads + hi)),
            pl.BlockSpec((1, s, HEAD_DIM), lambda bi, hi, qi: (bi, 0, 2 * n_heads + hi)),
            pl.BlockSpec((tk, 2 * tk), lambda bi, hi, qi: (0, 0)),
        ],
        out_specs=pl.BlockSpec((1, tq, HEAD_DIM), lambda bi, hi, qi: (bi, qi, hi)),
        out_shape=jax.ShapeDtypeStruct((b, s, n_heads * HEAD_DIM), BF16),
        compiler_params=_params("arbitrary", "arbitrary", "arbitrary"),
        name="sb_attention",
    )(proj, proj, proj, u2)


def _hgrn2_constants():
    n = HG_BLOCK
    r = np.arange(n)
    mats = [r[None, :] <= r[:, None],
            r[None, :] > r[:, None]]
    level = np.where(r[:, None] == r[None, :], 0, -1).astype(np.int32)
    for lv in range(1, HG_LEVELS + 1):
        half = n >> lv
        pos = r % (2 * half)
        bnd = r - pos + half - 1
        upper = pos >= half
        m_q = (r[None, :] > bnd[:, None]) & (r[None, :] <= r[:, None]) & upper[:, None]
        m_k = (r[None, :] > r[:, None]) & (r[None, :] <= bnd[:, None]) & ~upper[:, None]
        mats.append(m_q | m_k)
        same_group = (r[:, None] // (2 * half)) == (r[None, :] // (2 * half))
        pair = same_group & upper[:, None] & ~upper[None, :]
        level = np.where(pair, lv, level)
    m_all = np.concatenate(mats, axis=0).astype(np.float32)
    return jnp.asarray(m_all, BF16), jnp.asarray(level)


def _hgrn2_kernel(hq_ref, hf_ref, hi_ref, hg_ref, lbl_ref, ng_ref, mall_ref, lv_ref, o_ref, st_ref,
                  *, n_blocks, layer_j):
    n = HG_BLOCK
    logits = lbl_ref[...]
    e = jnp.exp(logits - jnp.max(logits, axis=0, keepdims=True))
    sm = e / jnp.sum(e, axis=0, keepdims=True)
    lb = jnp.sum(sm[: layer_j + 1], axis=0, keepdims=True)
    ng = ng_ref[...]
    st_ref[...] = jnp.zeros_like(st_ref)

    def block(bi, carry):
        r0 = pl.multiple_of(bi * n, n)
        rows = pl.ds(r0, n)
        hq = hq_ref[0, rows, :]
        f = lb + (1.0 - lb) * _sigmoid(hf_ref[0, rows, :])
        logf = jnp.log(f)
        kk = 1.0 - f
        q = hq * _sigmoid(hq)
        v = hi_ref[0, rows, :].astype(BF16)
        hi, lo = _split_bf16(logf)
        e2 = _dot(mall_ref[...], jnp.concatenate([hi, lo], axis=1))
        x = jnp.exp(e2[:, :HEAD_DIM] + e2[:, HEAD_DIM:])
        x_cum = x[0:n]
        x_tail = x[n:2 * n]
        st = st_ref[...]
        o = _dot_nt((q * x_cum).astype(BF16), st.astype(BF16))
        level = lv_ref[...]
        p = jnp.where(level == 0, _dot_nt(q.astype(BF16), kk.astype(BF16)), 0.0)
        for lv in range(1, HG_LEVELS + 1):
            xl = x[(lv + 1) * n:(lv + 2) * n]
            sc = _dot_nt((q * xl).astype(BF16), (kk * xl).astype(BF16))
            p = jnp.where(level == lv, sc, p)
        o = o + _dot(p.astype(BF16), v)
        k_dec = (kk * x_tail).astype(BF16)
        vt = hi_ref[0, rows, :].T.astype(BF16)
        st_ref[...] = st * x_cum[n - 1:n, :] + _dot(vt, k_dec)
        hg = hg_ref[0, rows, :]
        out = o * _rms_scale(o) * ng * (hg * _sigmoid(hg))
        o_ref[0, rows, :] = out.astype(o_ref.dtype)
        return carry

    lax.fori_loop(0, n_blocks, block, 0)


def _hgrn2(proj, lb_logits, norm_g, n_heads, col0, layer_j):
    b, s, _ = proj.shape
    assert s % HG_BLOCK == 0
    m_all, level = _hgrn2_constants()
    n_rows = lb_logits.shape[0]

    def col(off):
        return pl.BlockSpec((1, s, HEAD_DIM), lambda bi, hi: (bi, 0, col0 + off * n_heads + hi))

    kern = functools.partial(_hgrn2_kernel, n_blocks=s // HG_BLOCK, layer_j=layer_j)
    return pl.pallas_call(
        kern,
        grid=(b, n_heads),
        in_specs=[
            col(0), col(1), col(2), col(3),
            pl.BlockSpec((n_rows, HEAD_DIM), lambda bi, hi: (0, hi)),
            pl.BlockSpec((1, HEAD_DIM), lambda bi, hi: (0, 0)),
            pl.BlockSpec(m_all.shape, lambda bi, hi: (0, 0)),
            pl.BlockSpec(level.shape, lambda bi, hi: (0, 0)),
        ],
        out_specs=pl.BlockSpec((1, s, HEAD_DIM), lambda bi, hi: (bi, 0, hi)),
        out_shape=jax.ShapeDtypeStruct((b, s, n_heads * HEAD_DIM), BF16),
        scratch_shapes=[pltpu.VMEM((HEAD_DIM, HEAD_DIM), F32)],
        compiler_params=_params("arbitrary", "arbitrary"),
        name="hgrn2",
    )(proj, proj, proj, proj, lb_logits, norm_g.reshape(1, HEAD_DIM), m_all, level)


def _rglru_kernel(gate_ref, x_ref, cw_ref, cb_ref, wax_ref, bax_ref, lam_ref, o_ref, *, chunk, n_chunks):
    w = RG_BLOCK_WIDTH
    cw = cw_ref[...]
    cb = cb_ref[...]
    lam = lam_ref[...]
    neg_c_softplus = -RG_C * (jnp.maximum(-lam, 0.0) + jnp.log1p(jnp.exp(-jnp.abs(lam))))
    row = lax.broadcasted_iota(jnp.int32, (chunk, w), 0)
    sub = row & (SUBLANES - 1)

    def body(ci, carry):
        x_prev, h_prev = carry
        r0 = pl.multiple_of(ci * chunk, chunk)
        rows = pl.ds(r0, chunk)
        xc = x_ref[0, rows, :]
        xh = jnp.concatenate([x_prev, xc], axis=0)
        y = cb + xc * cw[0:1, :]
        for tap in range(1, CONV_WIDTH):
            y = y + pltpu.roll(xh, tap, axis=0)[SUBLANES:, :] * cw[tap:tap + 1, :]
        ax = _dot(y.astype(BF16), wax_ref[0]) + bax_ref[0]
        r = _sigmoid(ax[:, :w])
        gi = _sigmoid(ax[:, w:])
        log_a = r * neg_c_softplus
        a = jnp.exp(log_a)
        mult = jnp.sqrt(1.0 - a * a)
        mult = jnp.where(row + r0 == 0, 1.0, mult)
        u = y * gi * mult
        for shift in (1, 2, 4):
            keep = sub >= shift
            a_sh = pltpu.roll(a, shift, axis=0)
            u_sh = pltpu.roll(u, shift, axis=0)
            u = jnp.where(keep, u + a * u_sh, u)
            a = jnp.where(keep, a * a_sh, a)
        hs = []
        h_last = h_prev
        for gidx in range(chunk // SUBLANES):
            sl = slice(gidx * SUBLANES, (gidx + 1) * SUBLANES)
            hg = u[sl] + a[sl] * h_last
            hs.append(hg)
            h_last = hg[SUBLANES - 1:SUBLANES, :]
        h = jnp.concatenate(hs, axis=0)
        gt = gate_ref[0, rows, :]
        gelu = 0.5 * gt * (1.0 + jnp.tanh(np.sqrt(2.0 / np.pi) * (gt + 0.044715 * (gt * gt * gt))))
        o_ref[0, rows, :] = (gelu * h).astype(o_ref.dtype)
        return xc[chunk - SUBLANES:, :], h_last

    init = (jnp.zeros((SUBLANES, w), F32), jnp.zeros((1, w), F32))
    lax.fori_loop(0, n_chunks, body, init)


def _rglru(proj, conv_w, conv_b, wa, ba, wx, bx, lam, *, chunk_target=256):
    b, s, w2 = proj.shape
    lru = w2 // 2
    w = RG_BLOCK_WIDTH
    nb = lru // w
    chunk = _tile(s, chunk_target, SUBLANES)
    wax = jnp.concatenate([wa, wx], axis=-1).astype(BF16)
    bax = jnp.concatenate([ba, bx], axis=-1).reshape(nb, 1, 2 * w)
    kern = functools.partial(_rglru_kernel, chunk=chunk, n_chunks=s // chunk)
    return pl.pallas_call(
        kern,
        grid=(b, nb),
        in_specs=[
            pl.BlockSpec((1, s, w), lambda bi, ni: (bi, 0, ni)),
            pl.BlockSpec((1, s, w), lambda bi, ni: (bi, 0, nb + ni)),
            pl.BlockSpec((CONV_WIDTH, w), lambda bi, ni: (0, ni)),
            pl.BlockSpec((1, w), lambda bi, ni: (0, ni)),
            pl.BlockSpec((1, w, 2 * w), lambda bi, ni: (ni, 0, 0)),
            pl.BlockSpec((1, 1, 2 * w), lambda bi, ni: (ni, 0, 0)),
            pl.BlockSpec((1, w), lambda bi, ni: (0, ni)),
        ],
        out_specs=pl.BlockSpec((1, s, w), lambda bi, ni: (bi, 0, ni)),
        out_shape=jax.ShapeDtypeStruct((b, s, lru), BF16),
        compiler_params=_params("arbitrary", "arbitrary"),
        name="rglru",
    )(proj, proj, conv_w, conv_b.reshape(1, lru), wax, bax, lam.reshape(1, lru))


def kernel(x, p, mix_pre_g, mix_post_g, ffn_pre_g, ffn_post_g, ple_norm_g, w_in_even, w_out_even, hg_lb_logits, hg_norm_g, w_in_odd, conv_w, conv_b, rg_wa, rg_ba, rg_wx, rg_bx, rg_lambda, w_out_odd, w_gate_up, w_down, w_ple_up, w_ple_gate):
    b, s, d = x.shape
    t = b * s
    depth = p.shape[0]
    h = x.reshape(t, d)
    for i in range(depth):
        j = i // 2
        if i % 2 == 0:
            n_heads = w_out_even.shape[1] // (2 * HEAD_DIM)
            proj = _norm_matmul(h, mix_pre_g[i], w_in_even[j].astype(BF16), F32)
            proj = proj.reshape(b, s, -1)
            a_out = _sb_attention(proj, n_heads)
            b_out = _hgrn2(proj, hg_lb_logits, hg_norm_g[j], n_heads, 3 * n_heads, j)
            mixed = jnp.concatenate([a_out, b_out], axis=-1).reshape(t, -1)
            w_out = w_out_even[j].astype(BF16)
        else:
            proj = _norm_matmul(h, mix_pre_g[i], w_in_odd[j].astype(BF16), F32)
            mixed = _rglru(proj.reshape(b, s, -1), conv_w[j], conv_b[j], rg_wa[j], rg_ba[j],
                           rg_wx[j], rg_bx[j], rg_lambda[j]).reshape(t, -1)
            w_out = w_out_odd[j].astype(BF16)
        h = _matmul_norm_residual(mixed, w_out, mix_post_g[i], h, tk_target=2048)
        act = _norm_swiglu(h, ffn_pre_g[i], w_gate_up[i].astype(BF16))
        h = _matmul_norm_residual(act, w_down[i].astype(BF16), ffn_post_g[i], h)
        h = _per_layer_embedding(h, p[i].reshape(t, -1), w_ple_up[i].astype(BF16),
                                 w_ple_gate[i].astype(BF16), ple_norm_g[i])
    return h.reshape(b, s, d)
```

```python
import functools

import numpy as np
import jax
import jax.numpy as jnp
from jax import lax
from jax.experimental import pallas as pl
from jax.experimental.pallas import tpu as pltpu

F32 = jnp.float32
BF16 = jnp.bfloat16

RMS_EPS = 1e-6
LOG2_E = float(np.log2(np.e))
HEAD_DIM = 128
RG_BLOCK_WIDTH = 256
CONV_WIDTH = 4
RG_C = 8.0
V7X_VMEM_BYTES = 64 * 1024 * 1024
VMEM_LIMIT_BYTES = V7X_VMEM_BYTES - 8 * 1024 * 1024
SUBLANES = 8
HG_BLOCK = 128
HG_LEVELS = 7
HG_SAFE_EXPONENT = 80.0


def _tile(n, target, quantum):
    if n <= target:
        return n
    t = (target // quantum) * quantum
    while t > quantum and n % t:
        t -= quantum
    assert n % t == 0, (n, target, quantum)
    return t


def _params(*semantics):
    return pltpu.CompilerParams(dimension_semantics=semantics, vmem_limit_bytes=VMEM_LIMIT_BYTES)


def _sigmoid(x):
    return 1.0 / (1.0 + jnp.exp(-x))


def _rms_scale(x):
    return lax.rsqrt(jnp.mean(x * x, axis=-1, keepdims=True) + RMS_EPS)


def _dot(a, b):
    return jnp.dot(a, b, preferred_element_type=F32)


def _dot_nt(a, b):
    return lax.dot_general(a, b, (((1,), (1,)), ((), ())), preferred_element_type=F32)


def _neg_abs(x):
    bits = lax.bitcast_convert_type(x, jnp.uint32) | jnp.uint32(0x80000000)
    return lax.bitcast_convert_type(bits, F32)


def _split_bf16(x):
    hi = x.astype(BF16)
    lo = (x - hi.astype(F32)).astype(BF16)
    return hi, lo


def _norm_matmul_kernel(x_ref, g_ref, w_ref, o_ref, xn_ref):
    @pl.when(pl.program_id(1) == 0)
    def _():
        x = x_ref[...]
        xn_ref[...] = (x * _rms_scale(x) * g_ref[...]).astype(BF16)

    o_ref[...] = _dot(xn_ref[...], w_ref[...]).astype(o_ref.dtype)


def _norm_swiglu_kernel(x_ref, g_ref, wg_ref, wu_ref, o_ref, xn_ref):
    @pl.when(pl.program_id(1) == 0)
    def _():
        x = x_ref[...]
        xn_ref[...] = (x * _rms_scale(x) * g_ref[...]).astype(BF16)

    xn = xn_ref[...]
    gate = _dot(xn, wg_ref[...])
    up = _dot(xn, wu_ref[...])
    o_ref[...] = (gate * _sigmoid(gate) * up).astype(o_ref.dtype)


def _norm_matmul(x, g, w, out_dtype, *, tm_target=1024, tn_target=512):
    t, d = x.shape
    n = w.shape[1]
    tm = _tile(t, tm_target, SUBLANES)
    tn = _tile(n, tn_target, 128)
    return pl.pallas_call(
        _norm_matmul_kernel,
        grid=(t // tm, n // tn),
        in_specs=[
            pl.BlockSpec((tm, d), lambda i, j: (i, 0)),
            pl.BlockSpec((1, d), lambda i, j: (0, 0)),
            pl.BlockSpec((d, tn), lambda i, j: (0, j)),
        ],
        out_specs=pl.BlockSpec((tm, tn), lambda i, j: (i, j)),
        out_shape=jax.ShapeDtypeStruct((t, n), out_dtype),
        scratch_shapes=[pltpu.VMEM((tm, d), BF16)],
        compiler_params=_params("arbitrary", "arbitrary"),
        name="norm_matmul",
    )(x, g.reshape(1, d), w)


def _norm_swiglu(x, g, w_gate_up, *, tm_target=1024, tn_target=512):
    t, d = x.shape
    d_ff = w_gate_up.shape[1] // 2
    tm = _tile(t, tm_target, SUBLANES)
    tn = _tile(d_ff, tn_target, 128)
    nj = d_ff // tn
    return pl.pallas_call(
        _norm_swiglu_kernel,
        grid=(t // tm, nj),
        in_specs=[
            pl.BlockSpec((tm, d), lambda i, j: (i, 0)),
            pl.BlockSpec((1, d), lambda i, j: (0, 0)),
            pl.BlockSpec((d, tn), lambda i, j: (0, j)),
            pl.BlockSpec((d, tn), lambda i, j: (0, j + nj)),
        ],
        out_specs=pl.BlockSpec((tm, tn), lambda i, j: (i, j)),
        out_shape=jax.ShapeDtypeStruct((t, d_ff), BF16),
        scratch_shapes=[pltpu.VMEM((tm, d), BF16)],
        compiler_params=_params("arbitrary", "arbitrary"),
        name="norm_swiglu",
    )(x, g.reshape(1, d), w_gate_up, w_gate_up)


def _matmul_norm_residual_kernel(x_ref, w_ref, g_ref, h_ref, o_ref, acc_ref, *, nk):
    k = pl.program_id(1)
    part = _dot(x_ref[...].astype(BF16), w_ref[...])

    def finish(m):
        o_ref[...] = h_ref[...] + m * _rms_scale(m) * g_ref[...]

    if nk == 1:
        finish(part)
        return

    @pl.when(k == 0)
    def _():
        acc_ref[...] = part

    @pl.when(jnp.logical_and(k > 0, k < nk - 1))
    def _():
        acc_ref[...] += part

    @pl.when(k == nk - 1)
    def _():
        finish(acc_ref[...] + part)


def _matmul_norm_residual(x, w, g, h, *, tm_target=512, tk_target=512):
    t, kdim = x.shape
    d = w.shape[1]
    tm = _tile(t, tm_target, SUBLANES)
    tk = _tile(kdim, tk_target, 128)
    nk = kdim // tk
    return pl.pallas_call(
        functools.partial(_matmul_norm_residual_kernel, nk=nk),
        grid=(t // tm, nk),
        in_specs=[
            pl.BlockSpec((tm, tk), lambda i, k: (i, k)),
            pl.BlockSpec((tk, d), lambda i, k: (k, 0)),
            pl.BlockSpec((1, d), lambda i, k: (0, 0)),
            pl.BlockSpec((tm, d), lambda i, k: (i, 0)),
        ],
        out_specs=pl.BlockSpec((tm, d), lambda i, k: (i, 0)),
        out_shape=jax.ShapeDtypeStruct((t, d), F32),
        scratch_shapes=[pltpu.VMEM((tm, d), F32)],
        compiler_params=_params("arbitrary", "arbitrary"),
        name="matmul_norm_residual",
    )(x, w, g.reshape(1, d), h)


def _ple_kernel(h_ref, p_ref, wup_ref, wg_ref, g_ref, o_ref):
    h = h_ref[...]
    e = _dot(p_ref[...].astype(BF16), wup_ref[...])
    gate = _sigmoid(_dot(h.astype(BF16), wg_ref[...]))
    y = gate * e
    o_ref[...] = h + y * _rms_scale(y) * g_ref[...]


def _per_layer_embedding(h, p, w_up, w_gate, g, *, tm_target=256):
    t, d = h.shape
    pd = p.shape[1]
    tm = _tile(t, tm_target, SUBLANES)
    return pl.pallas_call(
        _ple_kernel,
        grid=(t // tm,),
        in_specs=[
            pl.BlockSpec((tm, d), lambda i: (i, 0)),
            pl.BlockSpec((tm, pd), lambda i: (i, 0)),
            pl.BlockSpec((pd, d), lambda i: (0, 0)),
            pl.BlockSpec((d, d), lambda i: (0, 0)),
            pl.BlockSpec((1, d), lambda i: (0, 0)),
        ],
        out_specs=pl.BlockSpec((tm, d), lambda i: (i, 0)),
        out_shape=jax.ShapeDtypeStruct((t, d), F32),
        compiler_params=_params("arbitrary"),
        name="per_layer_embedding",
    )(h, p, w_up, w_gate, g.reshape(1, d))


def _sb_attention_kernel(q_ref, k_ref, v_ref, u_ref, o_ref, *, tile, heads, scale):
    i = pl.program_id(2)
    u = u_ref[...]
    qs = [(q_ref[0, :, h * HEAD_DIM:(h + 1) * HEAD_DIM].astype(F32) * scale).astype(BF16)
          for h in range(heads)]
    t_idx = lax.broadcasted_iota(jnp.int32, (tile, tile), 0)
    s_idx = lax.broadcasted_iota(jnp.int32, (tile, tile), 1)
    below_diag = s_idx < t_idx

    def key_tile(j, state, masked):
        start = pl.multiple_of(j * tile, tile)
        hs = range(heads)
        cols = [slice(h * HEAD_DIM, (h + 1) * HEAD_DIM) for h in hs]
        z = [_dot_nt(qs[h], k_ref[0, pl.ds(start, tile), cols[h]]) for h in hs]
        sp = [jnp.maximum(z[h], 0.0) + jnp.log(1.0 + jnp.exp2(_neg_abs(z[h]))) * LOG2_E for h in hs]
        if masked:
            sp = [jnp.where(below_diag, sp[h], 0.0) for h in hs]
        split = [jnp.concatenate(_split_bf16(sp[h]), axis=1) for h in hs]
        log_rem = [_dot(split[h], u) for h in hs]
        w = [jnp.exp2((z[h] - sp[h]) + log_rem[h] + state[h][1]) for h in hs]
        if masked:
            w = [jnp.where(below_diag, w[h], 0.0) for h in hs]
        pv = [_dot(w[h].astype(BF16), v_ref[0, pl.ds(start, tile), cols[h]]) for h in hs]
        return tuple((state[h][0] + pv[h], state[h][1] - jnp.sum(sp[h], axis=1, keepdims=True)) for h in hs)

    state = tuple((jnp.zeros((tile, HEAD_DIM), F32), jnp.zeros((tile, 1), F32)) for _ in range(heads))
    state = key_tile(i, state, True)
    state = lax.fori_loop(0, i, lambda n, carry: key_tile(i - 1 - n, carry, False), state)
    for h in range(heads):
        o_ref[0, :, h * HEAD_DIM:(h + 1) * HEAD_DIM] = state[h][0].astype(o_ref.dtype)


def _sb_attention(proj, n_heads, *, tile_target=256, heads_per_step=4):
    b, s, _ = proj.shape
    tile = _tile(s, tile_target, 128)
    hp = heads_per_step if n_heads % heads_per_step == 0 else 1
    groups = n_heads // hp
    rows = np.arange(tile)
    neg_lower = -(rows[:, None] > rows[None, :]).astype(np.float32)
    u = jnp.asarray(np.concatenate([neg_lower, neg_lower], axis=0), BF16)
    kern = functools.partial(_sb_attention_kernel, tile=tile, heads=hp, scale=HEAD_DIM ** -0.5 * LOG2_E)
    width = hp * HEAD_DIM
    return pl.pallas_call(
        kern,
        grid=(b, groups, s // tile),
        in_specs=[
            pl.BlockSpec((1, tile, width), lambda bi, gi, qi: (bi, qi, gi)),
            pl.BlockSpec((1, s, width), lambda bi, gi, qi: (bi, 0, groups + gi)),
            pl.BlockSpec((1, s, width), lambda bi, gi, qi: (bi, 0, 2 * groups + gi)),
            pl.BlockSpec((2 * tile, tile), lambda bi, gi, qi: (0, 0)),
        ],
        out_specs=pl.BlockSpec((1, tile, width), lambda bi, gi, qi: (bi, qi, gi)),
        out_shape=jax.ShapeDtypeStruct((b, s, n_heads * HEAD_DIM), BF16),
        compiler_params=_params("arbitrary", "arbitrary", "arbitrary"),
        name="sb_attention",
    )(proj, proj, proj, u)


def _hgrn2_constants():
    n = HG_BLOCK
    r = np.arange(n)
    tri = (r[None, :] <= r[:, None])
    mats = [r[None, :] > r[:, None]]
    level = np.where(r[:, None] == r[None, :], 0, -1).astype(np.int32)
    for lv in range(1, HG_LEVELS + 1):
        half = n >> lv
        pos = r % (2 * half)
        bnd = r - pos + half - 1
        upper = pos >= half
        m_q = (r[None, :] > bnd[:, None]) & (r[None, :] <= r[:, None]) & upper[:, None]
        m_k = (r[None, :] > r[:, None]) & (r[None, :] <= bnd[:, None]) & ~upper[:, None]
        mats.append(m_q | m_k)
        same_group = (r[:, None] // (2 * half)) == (r[None, :] // (2 * half))
        pair = same_group & upper[:, None] & ~upper[None, :]
        level = np.where(pair, lv, level)
    m_lev = np.concatenate(mats, axis=0).astype(np.float32)
    return jnp.asarray(tri.astype(np.float32), BF16), jnp.asarray(m_lev, BF16), jnp.asarray(level)


def _hgrn2_kernel(hq_ref, hf_ref, hi_ref, hg_ref, lbl_ref, ng_ref, tri_ref, mlev_ref, lv_ref, o_ref, st_ref,
                  *, n_blocks, layer_j, heads):
    n = HG_BLOCK
    mid_row = n // 2 - 1
    logits = lbl_ref[...]
    e = jnp.exp(logits - jnp.max(logits, axis=0, keepdims=True))
    sm = e / jnp.sum(e, axis=0, keepdims=True)
    lb_all = jnp.sum(sm[: layer_j + 1], axis=0, keepdims=True)
    ng = ng_ref[...]
    st_ref[...] = jnp.zeros_like(st_ref)

    def block(bi, carry):
        r0 = pl.multiple_of(bi * n, n)
        rows = pl.ds(r0, n)
        level = lv_ref[...]

        hs = range(heads)
        col_of = [slice(h * HEAD_DIM, (h + 1) * HEAD_DIM) for h in hs]
        f = [lb_all[:, col_of[h]] + (1.0 - lb_all[:, col_of[h]]) * _sigmoid(hf_ref[0, rows, col_of[h]].astype(F32))
             for h in hs]
        g2 = [jnp.concatenate(_split_bf16(jnp.log(f[h])), axis=1) for h in hs]
        c2 = [_dot(tri_ref[...], g2[h]) for h in hs]
        cum = [c2[h][:, :HEAD_DIM] + c2[h][:, HEAD_DIM:] for h in hs]
        mid = [cum[h][mid_row:mid_row + 1, :] for h in hs]
        last = [cum[h][n - 1:n, :] for h in hs]
        worst = functools.reduce(jnp.minimum, [jnp.minimum(mid[h], last[h] - mid[h]) for h in hs])
        safe = jnp.min(worst) >= -HG_SAFE_EXPONENT
        kk = [1.0 - f[h] for h in hs]
        hq = [hq_ref[0, rows, col_of[h]].astype(F32) for h in hs]
        q = [hq[h] * _sigmoid(hq[h]) for h in hs]
        pre = [(col_of[h], q[h], kk[h], g2[h], cum[h], mid[h], last[h]) for h in hs]

        def finish(h, cols, o):
            hg = hg_ref[0, rows, cols].astype(F32)
            out = o * _rms_scale(o) * ng * (hg * _sigmoid(hg))
            o_ref[0, rows, cols] = out.astype(o_ref.dtype)

        @pl.when(safe)
        def _():
            v = [hi_ref[0, rows, col_of[h]] for h in hs]
            qd = [(q[h] * jnp.exp(cum[h] - mid[h])).astype(BF16) for h in hs]
            kd = [kk[h] * jnp.exp(mid[h] - cum[h]) for h in hs]
            st = [st_ref[h] for h in hs]
            o = [_dot_nt(qd[h], (st[h] * jnp.exp(mid[h])).astype(BF16)) for h in hs]
            sc = [_dot_nt(qd[h], kd[h].astype(BF16)) for h in hs]
            p = [jnp.where(level >= 0, sc[h], 0.0).astype(BF16) for h in hs]
            o = [o[h] + _dot(p[h], v[h]) for h in hs]
            k_dec = [(kd[h] * jnp.exp(last[h] - mid[h])).astype(BF16) for h in hs]
            vt = [v[h].astype(F32).T.astype(BF16) for h in hs]
            for h in hs:
                st_ref[h] = st[h] * jnp.exp(last[h]) + _dot(vt[h], k_dec[h])
                finish(h, col_of[h], o[h])

        @pl.when(jnp.logical_not(safe))
        def _():
            for h, (cols, q, kk, g2, cum, mid, last) in enumerate(pre):
                v = hi_ref[0, rows, cols]
                e2 = _dot(mlev_ref[...], g2)
                x = jnp.exp(e2[:, :HEAD_DIM] + e2[:, HEAD_DIM:])
                st = st_ref[h]
                o = _dot_nt((q * jnp.exp(cum)).astype(BF16), st.astype(BF16))
                p = jnp.where(level == 0, _dot_nt(q.astype(BF16), kk.astype(BF16)), 0.0)
                for lv in range(1, HG_LEVELS + 1):
                    xl = x[lv * n:(lv + 1) * n]
                    sc = _dot_nt((q * xl).astype(BF16), (kk * xl).astype(BF16))
                    p = jnp.where(level == lv, sc, p)
                o = o + _dot(p.astype(BF16), v)
                k_dec = (kk * x[0:n]).astype(BF16)
                vt = v.astype(F32).T.astype(BF16)
                st_ref[h] = st * jnp.exp(last) + _dot(vt, k_dec)
                finish(h, cols, o)

        return carry

    lax.fori_loop(0, n_blocks, block, 0)


def _hgrn2(proj, lb_logits, norm_g, n_heads, col0, layer_j, *, heads_per_step=4):
    b, s, _ = proj.shape
    assert s % HG_BLOCK == 0
    hp = heads_per_step if (n_heads % heads_per_step == 0 and col0 % heads_per_step == 0) else 1
    groups = n_heads // hp
    width = hp * HEAD_DIM
    tri, m_lev, level = _hgrn2_constants()
    n_rows = lb_logits.shape[0]

    def col(off):
        return pl.BlockSpec((1, s, width), lambda bi, gi: (bi, 0, col0 // hp + off * groups + gi))

    kern = functools.partial(_hgrn2_kernel, n_blocks=s // HG_BLOCK, layer_j=layer_j, heads=hp)
    return pl.pallas_call(
        kern,
        grid=(b, groups),
        in_specs=[
            col(0), col(1), col(2), col(3),
            pl.BlockSpec((n_rows, width), lambda bi, gi: (0, gi)),
            pl.BlockSpec((1, HEAD_DIM), lambda bi, gi: (0, 0)),
            pl.BlockSpec(tri.shape, lambda bi, gi: (0, 0)),
            pl.BlockSpec(m_lev.shape, lambda bi, gi: (0, 0)),
            pl.BlockSpec(level.shape, lambda bi, gi: (0, 0)),
        ],
        out_specs=pl.BlockSpec((1, s, width), lambda bi, gi: (bi, 0, gi)),
        out_shape=jax.ShapeDtypeStruct((b, s, n_heads * HEAD_DIM), BF16),
        scratch_shapes=[pltpu.VMEM((hp, HEAD_DIM, HEAD_DIM), F32)],
        compiler_params=_params("arbitrary", "arbitrary"),
        name="hgrn2",
    )(proj, proj, proj, proj, lb_logits, norm_g.reshape(1, HEAD_DIM), tri, m_lev, level)


def _rglru_kernel(gate_ref, x_ref, cw_ref, cb_ref, wax_ref, bax_ref, lam_ref, o_ref, *, chunk, n_chunks):
    w = RG_BLOCK_WIDTH
    cw = cw_ref[...]
    cb = cb_ref[...]
    lam = lam_ref[...]
    neg_c_softplus = -RG_C * (jnp.maximum(-lam, 0.0) + jnp.log1p(jnp.exp(-jnp.abs(lam))))
    row = lax.broadcasted_iota(jnp.int32, (chunk, w), 0)
    sub = row & (SUBLANES - 1)

    def body(ci, carry):
        x_prev, h_prev = carry
        r0 = pl.multiple_of(ci * chunk, chunk)
        rows = pl.ds(r0, chunk)
        xc = x_ref[0, rows, :].astype(F32)
        xh = jnp.concatenate([x_prev, xc], axis=0)
        y = cb + xc * cw[0:1, :]
        for tap in range(1, CONV_WIDTH):
            y = y + pltpu.roll(xh, tap, axis=0)[SUBLANES:, :] * cw[tap:tap + 1, :]
        ax = _dot(y.astype(BF16), wax_ref[0]) + bax_ref[0]
        r = _sigmoid(ax[:, :w])
        gi = _sigmoid(ax[:, w:])
        log_a = r * neg_c_softplus
        a = jnp.exp(log_a)
        mult = jnp.sqrt(1.0 - a * a)
        mult = jnp.where(row + r0 == 0, 1.0, mult)
        u = y * gi * mult
        for shift in (1, 2, 4):
            keep = sub >= shift
            a_sh = pltpu.roll(a, shift, axis=0)
            u_sh = pltpu.roll(u, shift, axis=0)
            u = jnp.where(keep, u + a * u_sh, u)
            a = jnp.where(keep, a * a_sh, a)
        hs = []
        h_last = h_prev
        for gidx in range(chunk // SUBLANES):
            sl = slice(gidx * SUBLANES, (gidx + 1) * SUBLANES)
            hg = u[sl] + a[sl] * h_last
            hs.append(hg)
            h_last = hg[SUBLANES - 1:SUBLANES, :]
        h = jnp.concatenate(hs, axis=0)
        gt = gate_ref[0, rows, :].astype(F32)
        gelu = 0.5 * gt * (1.0 + jnp.tanh(np.sqrt(2.0 / np.pi) * (gt + 0.044715 * (gt * gt * gt))))
        o_ref[0, rows, :] = (gelu * h).astype(o_ref.dtype)
        return xc[chunk - SUBLANES:, :], h_last

    init = (jnp.zeros((SUBLANES, w), F32), jnp.zeros((1, w), F32))
    lax.fori_loop(0, n_chunks, body, init)


def _rglru(proj, conv_w, conv_b, wa, ba, wx, bx, lam, *, chunk_target=256):
    b, s, w2 = proj.shape
    lru = w2 // 2
    w = RG_BLOCK_WIDTH
    nb = lru // w
    chunk = _tile(s, chunk_target, 2 * SUBLANES)
    wax = jnp.concatenate([wa, wx], axis=-1).astype(BF16)
    bax = jnp.concatenate([ba, bx], axis=-1).reshape(nb, 1, 2 * w)
    kern = functools.partial(_rglru_kernel, chunk=chunk, n_chunks=s // chunk)
    return pl.pallas_call(
        kern,
        grid=(b, nb),
        in_specs=[
            pl.BlockSpec((1, s, w), lambda bi, ni: (bi, 0, ni)),
            pl.BlockSpec((1, s, w), lambda bi, ni: (bi, 0, nb + ni)),
            pl.BlockSpec((CONV_WIDTH, w), lambda bi, ni: (0, ni)),
            pl.BlockSpec((1, w), lambda bi, ni: (0, ni)),
            pl.BlockSpec((1, w, 2 * w), lambda bi, ni: (ni, 0, 0)),
            pl.BlockSpec((1, 1, 2 * w), lambda bi, ni: (ni, 0, 0)),
            pl.BlockSpec((1, w), lambda bi, ni: (0, ni)),
        ],
        out_specs=pl.BlockSpec((1, s, w), lambda bi, ni: (bi, 0, ni)),
        out_shape=jax.ShapeDtypeStruct((b, s, lru), BF16),
        compiler_params=_params("arbitrary", "arbitrary"),
        name="rglru",
    )(proj, proj, conv_w, conv_b.reshape(1, lru), wax, bax, lam.reshape(1, lru))


def kernel(x, p, mix_pre_g, mix_post_g, ffn_pre_g, ffn_post_g, ple_norm_g, w_in_even, w_out_even, hg_lb_logits, hg_norm_g, w_in_odd, conv_w, conv_b, rg_wa, rg_ba, rg_wx, rg_bx, rg_lambda, w_out_odd, w_gate_up, w_down, w_ple_up, w_ple_gate):
    b, s, d = x.shape
    t = b * s
    depth = p.shape[0]
    h = x.reshape(t, d)
    for i in range(depth):
        j = i // 2
        if i % 2 == 0:
            n_heads = w_out_even.shape[1] // (2 * HEAD_DIM)
            proj = _norm_matmul(h, mix_pre_g[i], w_in_even[j].astype(BF16), BF16)
            proj = proj.reshape(b, s, -1)
            a_out = _sb_attention(proj, n_heads)
            b_out = _hgrn2(proj, hg_lb_logits, hg_norm_g[j], n_heads, 3 * n_heads, j)
            mixed = jnp.concatenate([a_out, b_out], axis=-1).reshape(t, -1)
            w_out = w_out_even[j].astype(BF16)
        else:
            proj = _norm_matmul(h, mix_pre_g[i], w_in_odd[j].astype(BF16), BF16)
            mixed = _rglru(proj.reshape(b, s, -1), conv_w[j], conv_b[j], rg_wa[j], rg_ba[j],
                           rg_wx[j], rg_bx[j], rg_lambda[j]).reshape(t, -1)
            w_out = w_out_odd[j].astype(BF16)
        h = _matmul_norm_residual(mixed, w_out, mix_post_g[i], h, tk_target=2048)
        act = _norm_swiglu(h, ffn_pre_g[i], w_gate_up[i].astype(BF16))
        h = _matmul_norm_residual(act, w_down[i].astype(BF16), ffn_post_g[i], h)
        h = _per_layer_embedding(h, p[i].reshape(t, -1), w_ple_up[i].astype(BF16),
                                 w_ple_gate[i].astype(BF16), ple_norm_g[i])
    return h.reshape(b, s, d)
```

```python
import functools

import numpy as np
import jax
import jax.numpy as jnp
from jax import lax
from jax.experimental import pallas as pl
from jax.experimental.pallas import tpu as pltpu

F32 = jnp.float32
BF16 = jnp.bfloat16

RMS_EPS = 1e-6
LOG2_E = float(np.log2(np.e))
HEAD_DIM = 128
RG_BLOCK_WIDTH = 256
CONV_WIDTH = 4
RG_C = 8.0
V7X_VMEM_BYTES = 64 * 1024 * 1024
VMEM_LIMIT_BYTES = V7X_VMEM_BYTES - 8 * 1024 * 1024
SUBLANES = 8
HG_BLOCK = 128
HG_LEVELS = 7
HG_SAFE_EXPONENT = 80.0


def _tile(n, target, quantum):
    if n <= target:
        return n
    t = (target // quantum) * quantum
    while t > quantum and n % t:
        t -= quantum
    assert n % t == 0, (n, target, quantum)
    return t


def _params(*semantics):
    return pltpu.CompilerParams(dimension_semantics=semantics, vmem_limit_bytes=VMEM_LIMIT_BYTES)


def _sigmoid(x):
    return 0.5 + 0.5 * jnp.tanh(0.5 * x)


def _rms_scale(x):
    return lax.rsqrt(jnp.mean(x * x, axis=-1, keepdims=True) + RMS_EPS)


def _dot(a, b):
    return jnp.dot(a, b, preferred_element_type=F32)


def _dot_nt(a, b):
    return lax.dot_general(a, b, (((1,), (1,)), ((), ())), preferred_element_type=F32)


def _split_bf16(x):
    hi = x.astype(BF16)
    lo = (x - hi.astype(F32)).astype(BF16)
    return hi, lo


def _norm_matmul_kernel(x_ref, g_ref, w_ref, o_ref, xn_ref):
    @pl.when(pl.program_id(1) == 0)
    def _():
        x = x_ref[...]
        xn_ref[...] = (x * _rms_scale(x) * g_ref[...]).astype(BF16)

    o_ref[...] = _dot(xn_ref[...], w_ref[...].astype(BF16)).astype(o_ref.dtype)


def _norm_swiglu_kernel(x_ref, g_ref, wg_ref, wu_ref, o_ref, xn_ref):
    @pl.when(pl.program_id(1) == 0)
    def _():
        x = x_ref[...]
        xn_ref[...] = (x * _rms_scale(x) * g_ref[...]).astype(BF16)

    xn = xn_ref[...]
    gate = _dot(xn, wg_ref[...].astype(BF16))
    up = _dot(xn, wu_ref[...].astype(BF16))
    o_ref[...] = (gate * _sigmoid(gate) * up).astype(o_ref.dtype)


def _norm_matmul(x, g, w, layer, out_dtype, *, tm_target=1024, tn_target=512):
    t, d = x.shape
    n = w.shape[2]
    tm = _tile(t, tm_target, SUBLANES)
    tn = _tile(n, tn_target, 128)
    return pl.pallas_call(
        _norm_matmul_kernel,
        grid=(t // tm, n // tn),
        in_specs=[
            pl.BlockSpec((tm, d), lambda i, j: (i, 0)),
            pl.BlockSpec((1, d), lambda i, j: (0, 0)),
            pl.BlockSpec((None, d, tn), lambda i, j: (layer, 0, j)),
        ],
        out_specs=pl.BlockSpec((tm, tn), lambda i, j: (i, j)),
        out_shape=jax.ShapeDtypeStruct((t, n), out_dtype),
        scratch_shapes=[pltpu.VMEM((tm, d), BF16)],
        compiler_params=_params("arbitrary", "arbitrary"),
        name="norm_matmul",
    )(x, g.reshape(1, d), w)


def _norm_swiglu(x, g, w_gate_up, layer, *, tm_target=1024, tn_target=512):
    t, d = x.shape
    d_ff = w_gate_up.shape[2] // 2
    tm = _tile(t, tm_target, SUBLANES)
    tn = _tile(d_ff, tn_target, 128)
    nj = d_ff // tn
    return pl.pallas_call(
        _norm_swiglu_kernel,
        grid=(t // tm, nj),
        in_specs=[
            pl.BlockSpec((tm, d), lambda i, j: (i, 0)),
            pl.BlockSpec((1, d), lambda i, j: (0, 0)),
            pl.BlockSpec((None, d, tn), lambda i, j: (layer, 0, j)),
            pl.BlockSpec((None, d, tn), lambda i, j: (layer, 0, j + nj)),
        ],
        out_specs=pl.BlockSpec((tm, tn), lambda i, j: (i, j)),
        out_shape=jax.ShapeDtypeStruct((t, d_ff), BF16),
        scratch_shapes=[pltpu.VMEM((tm, d), BF16)],
        compiler_params=_params("arbitrary", "arbitrary"),
        name="norm_swiglu",
    )(x, g.reshape(1, d), w_gate_up, w_gate_up)


def _matmul_norm_residual_kernel(x_ref, w_ref, g_ref, h_ref, o_ref, acc_ref, *, nk):
    k = pl.program_id(1)

    def finish(m):
        o_ref[...] = h_ref[...] + m * _rms_scale(m) * g_ref[...]

    if nk == 1:
        finish(_dot(x_ref[...], w_ref[...]))
        return

    @pl.when(k == 0)
    def _():
        acc_ref[...] = jnp.zeros_like(acc_ref)

    acc_ref[...] += _dot(x_ref[...], w_ref[...])

    @pl.when(k == nk - 1)
    def _():
        finish(acc_ref[...])


def _matmul_norm_residual(x, w, g, h, *, tm_target=512, tk_target=1408):
    t, kdim = x.shape
    d = w.shape[1]
    tm = _tile(t, tm_target, SUBLANES)
    tk = _tile(kdim, tk_target, 128)
    nk = kdim // tk
    return pl.pallas_call(
        functools.partial(_matmul_norm_residual_kernel, nk=nk),
        grid=(t // tm, nk),
        in_specs=[
            pl.BlockSpec((tm, tk), lambda i, k: (i, k)),
            pl.BlockSpec((tk, d), lambda i, k: (k, 0)),
            pl.BlockSpec((1, d), lambda i, k: (0, 0)),
            pl.BlockSpec((tm, d), lambda i, k: (i, 0)),
        ],
        out_specs=pl.BlockSpec((tm, d), lambda i, k: (i, 0)),
        out_shape=jax.ShapeDtypeStruct((t, d), F32),
        scratch_shapes=[pltpu.VMEM((tm, d), F32)],
        compiler_params=_params("arbitrary", "arbitrary"),
        name="matmul_norm_residual",
    )(x, w, g.reshape(1, d), h)


def _ple_kernel(h_ref, p_ref, wup_ref, wg_ref, g_ref, o_ref):
    h = h_ref[...]
    e = _dot(p_ref[...].astype(BF16), wup_ref[...])
    gate = _sigmoid(_dot(h.astype(BF16), wg_ref[...]))
    y = gate * e
    o_ref[...] = h + y * _rms_scale(y) * g_ref[...]


def _per_layer_embedding(h, p, layer, w_up, w_gate, g, *, tm_target=512):
    t, d = h.shape
    pd = p.shape[2]
    tm = _tile(t, tm_target, SUBLANES)
    return pl.pallas_call(
        _ple_kernel,
        grid=(t // tm,),
        in_specs=[
            pl.BlockSpec((tm, d), lambda i: (i, 0)),
            pl.BlockSpec((None, tm, pd), lambda i: (layer, i, 0)),
            pl.BlockSpec((pd, d), lambda i: (0, 0)),
            pl.BlockSpec((d, d), lambda i: (0, 0)),
            pl.BlockSpec((1, d), lambda i: (0, 0)),
        ],
        out_specs=pl.BlockSpec((tm, d), lambda i: (i, 0)),
        out_shape=jax.ShapeDtypeStruct((t, d), F32),
        compiler_params=_params("arbitrary"),
        name="per_layer_embedding",
    )(h, p, w_up, w_gate, g.reshape(1, d))


def _sb_attention_kernel(q_ref, k_ref, v_ref, u_ref, o_ref, *, tile, heads, scale):
    i = pl.program_id(2)
    u = u_ref[...]
    qs = [(q_ref[0, :, h * HEAD_DIM:(h + 1) * HEAD_DIM].astype(F32) * scale).astype(BF16)
          for h in range(heads)]
    t_idx = lax.broadcasted_iota(jnp.int32, (tile, tile), 0)
    s_idx = lax.broadcasted_iota(jnp.int32, (tile, tile), 1)
    below_diag = s_idx < t_idx

    def key_tile(j, state, masked):
        start = pl.multiple_of(j * tile, tile)
        hs = range(heads)
        cols = [slice(h * HEAD_DIM, (h + 1) * HEAD_DIM) for h in hs]
        z = [_dot_nt(qs[h], k_ref[0, pl.ds(start, tile), cols[h]]) for h in hs]
        sp = [jnp.maximum(z[h], 0.0) + jnp.log(1.0 + jnp.exp2(-jnp.abs(z[h]))) * LOG2_E for h in hs]
        if masked:
            sp = [jnp.where(below_diag, sp[h], 0.0) for h in hs]
        split = [jnp.concatenate(_split_bf16(sp[h]), axis=1) for h in hs]
        log_rem = [_dot(split[h], u) for h in hs]
        w = [jnp.exp2((z[h] - sp[h]) + log_rem[h] + state[h][1]) for h in hs]
        if masked:
            w = [jnp.where(below_diag, w[h], 0.0) for h in hs]
        pv = [_dot(w[h].astype(BF16), v_ref[0, pl.ds(start, tile), cols[h]]) for h in hs]
        return tuple((state[h][0] + pv[h], state[h][1] - jnp.sum(sp[h], axis=1, keepdims=True)) for h in hs)

    state = tuple((jnp.zeros((tile, HEAD_DIM), F32), jnp.zeros((tile, 1), F32)) for _ in range(heads))
    state = key_tile(i, state, True)
    state = lax.fori_loop(0, i, lambda n, carry: key_tile(i - 1 - n, carry, False), state)
    for h in range(heads):
        o_ref[0, :, h * HEAD_DIM:(h + 1) * HEAD_DIM] = state[h][0].astype(o_ref.dtype)


def _sb_attention(proj, n_heads, *, tile_target=256, heads_per_step=4):
    b, s, _ = proj.shape
    tile = _tile(s, tile_target, 128)
    hp = heads_per_step if n_heads % heads_per_step == 0 else 1
    groups = n_heads // hp
    rows = np.arange(tile)
    neg_lower = -(rows[:, None] > rows[None, :]).astype(np.float32)
    u = jnp.asarray(np.concatenate([neg_lower, neg_lower], axis=0), BF16)
    kern = functools.partial(_sb_attention_kernel, tile=tile, heads=hp, scale=HEAD_DIM ** -0.5 * LOG2_E)
    width = hp * HEAD_DIM
    return pl.pallas_call(
        kern,
        grid=(b, groups, s // tile),
        in_specs=[
            pl.BlockSpec((1, tile, width), lambda bi, gi, qi: (bi, qi, gi)),
            pl.BlockSpec((1, s, width), lambda bi, gi, qi: (bi, 0, groups + gi)),
            pl.BlockSpec((1, s, width), lambda bi, gi, qi: (bi, 0, 2 * groups + gi)),
            pl.BlockSpec((2 * tile, tile), lambda bi, gi, qi: (0, 0)),
        ],
        out_specs=pl.BlockSpec((1, tile, width), lambda bi, gi, qi: (bi, qi, gi)),
        out_shape=jax.ShapeDtypeStruct((b, s, n_heads * HEAD_DIM), BF16),
        compiler_params=_params("arbitrary", "arbitrary", "arbitrary"),
        name="sb_attention",
    )(proj, proj, proj, u)


def _hgrn2_constants():
    n = HG_BLOCK
    r = np.arange(n)
    tri = (r[None, :] <= r[:, None])
    mats = [r[None, :] > r[:, None]]
    level = np.where(r[:, None] == r[None, :], 0, -1).astype(np.int32)
    for lv in range(1, HG_LEVELS + 1):
        half = n >> lv
        pos = r % (2 * half)
        bnd = r - pos + half - 1
        upper = pos >= half
        m_q = (r[None, :] > bnd[:, None]) & (r[None, :] <= r[:, None]) & upper[:, None]
        m_k = (r[None, :] > r[:, None]) & (r[None, :] <= bnd[:, None]) & ~upper[:, None]
        mats.append(m_q | m_k)
        same_group = (r[:, None] // (2 * half)) == (r[None, :] // (2 * half))
        pair = same_group & upper[:, None] & ~upper[None, :]
        level = np.where(pair, lv, level)
    m_lev = np.concatenate(mats, axis=0).astype(np.float32)
    return jnp.asarray(tri.astype(np.float32), BF16), jnp.asarray(m_lev, BF16), jnp.asarray(level)


def _hgrn2_kernel(hq_ref, hf_ref, hi_ref, hg_ref, lbl_ref, ng_ref, tri_ref, mlev_ref, lv_ref, o_ref, st_ref,
                  *, n_blocks, layer_j, heads):
    n = HG_BLOCK
    mid_row = n // 2 - 1
    logits = lbl_ref[...]
    e = jnp.exp(logits - jnp.max(logits, axis=0, keepdims=True))
    sm = e / jnp.sum(e, axis=0, keepdims=True)
    lb_all = jnp.sum(sm[: layer_j + 1], axis=0, keepdims=True)
    ng = ng_ref[...]
    st_ref[...] = jnp.zeros_like(st_ref)

    def block(bi, carry):
        r0 = pl.multiple_of(bi * n, n)
        rows = pl.ds(r0, n)
        level = lv_ref[...]

        hs = range(heads)
        col_of = [slice(h * HEAD_DIM, (h + 1) * HEAD_DIM) for h in hs]
        f = [lb_all[:, col_of[h]] + (1.0 - lb_all[:, col_of[h]]) * _sigmoid(hf_ref[0, rows, col_of[h]].astype(F32))
             for h in hs]
        g2 = [jnp.concatenate(_split_bf16(jnp.log(f[h])), axis=1) for h in hs]
        c2 = [_dot(tri_ref[...], g2[h]) for h in hs]
        cum = [c2[h][:, :HEAD_DIM] + c2[h][:, HEAD_DIM:] for h in hs]
        mid = [cum[h][mid_row:mid_row + 1, :] for h in hs]
        last = [cum[h][n - 1:n, :] for h in hs]
        worst = functools.reduce(jnp.minimum, [jnp.minimum(mid[h], last[h] - mid[h]) for h in hs])
        safe = jnp.min(worst) >= -HG_SAFE_EXPONENT
        kk = [1.0 - f[h] for h in hs]
        hq = [hq_ref[0, rows, col_of[h]].astype(F32) for h in hs]
        q = [hq[h] * _sigmoid(hq[h]) for h in hs]
        pre = [(col_of[h], q[h], kk[h], g2[h], cum[h], mid[h], last[h]) for h in hs]

        def finish(h, cols, o):
            hg = hg_ref[0, rows, cols].astype(F32)
            out = o * _rms_scale(o) * ng * (hg * _sigmoid(hg))
            o_ref[0, rows, cols] = out.astype(o_ref.dtype)

        @pl.when(safe)
        def _():
            v = [hi_ref[0, rows, col_of[h]] for h in hs]
            qd = [(q[h] * jnp.exp(cum[h] - mid[h])).astype(BF16) for h in hs]
            kd = [kk[h] * jnp.exp(mid[h] - cum[h]) for h in hs]
            st = [st_ref[h] for h in hs]
            o = [_dot_nt(qd[h], (st[h] * jnp.exp(mid[h])).astype(BF16)) for h in hs]
            sc = [_dot_nt(qd[h], kd[h].astype(BF16)) for h in hs]
            p = [jnp.where(level >= 0, sc[h], 0.0).astype(BF16) for h in hs]
            o = [o[h] + _dot(p[h], v[h]) for h in hs]
            k_dec = [(kd[h] * jnp.exp(last[h] - mid[h])).astype(BF16) for h in hs]
            vt = [v[h].astype(F32).T.astype(BF16) for h in hs]
            for h in hs:
                st_ref[h] = st[h] * jnp.exp(last[h]) + _dot(vt[h], k_dec[h])
                finish(h, col_of[h], o[h])

        @pl.when(jnp.logical_not(safe))
        def _():
            for h, (cols, q, kk, g2, cum, mid, last) in enumerate(pre):
                v = hi_ref[0, rows, cols]
                e2 = _dot(mlev_ref[...], g2)
                x = jnp.exp(e2[:, :HEAD_DIM] + e2[:, HEAD_DIM:])
                st = st_ref[h]
                o = _dot_nt((q * jnp.exp(cum)).astype(BF16), st.astype(BF16))
                p = jnp.where(level == 0, _dot_nt(q.astype(BF16), kk.astype(BF16)), 0.0)
                for lv in range(1, HG_LEVELS + 1):
                    xl = x[lv * n:(lv + 1) * n]
                    sc = _dot_nt((q * xl).astype(BF16), (kk * xl).astype(BF16))
                    p = jnp.where(level == lv, sc, p)
                o = o + _dot(p.astype(BF16), v)
                k_dec = (kk * x[0:n]).astype(BF16)
                vt = v.astype(F32).T.astype(BF16)
                st_ref[h] = st * jnp.exp(last) + _dot(vt, k_dec)
                finish(h, cols, o)

        return carry

    lax.fori_loop(0, n_blocks, block, 0)


def _hgrn2(proj, lb_logits, norm_g, n_heads, col0, layer_j, *, heads_per_step=4):
    b, s, _ = proj.shape
    assert s % HG_BLOCK == 0
    hp = heads_per_step if (n_heads % heads_per_step == 0 and col0 % heads_per_step == 0) else 1
    groups = n_heads // hp
    width = hp * HEAD_DIM
    tri, m_lev, level = _hgrn2_constants()
    n_rows = lb_logits.shape[0]

    def col(off):
        return pl.BlockSpec((1, s, width), lambda bi, gi: (bi, 0, col0 // hp + off * groups + gi))

    kern = functools.partial(_hgrn2_kernel, n_blocks=s // HG_BLOCK, layer_j=layer_j, heads=hp)
    return pl.pallas_call(
        kern,
        grid=(b, groups),
        in_specs=[
            col(0), col(1), col(2), col(3),
            pl.BlockSpec((n_rows, width), lambda bi, gi: (0, gi)),
            pl.BlockSpec((1, HEAD_DIM), lambda bi, gi: (0, 0)),
            pl.BlockSpec(tri.shape, lambda bi, gi: (0, 0)),
            pl.BlockSpec(m_lev.shape, lambda bi, gi: (0, 0)),
            pl.BlockSpec(level.shape, lambda bi, gi: (0, 0)),
        ],
        out_specs=pl.BlockSpec((1, s, width), lambda bi, gi: (bi, 0, gi)),
        out_shape=jax.ShapeDtypeStruct((b, s, n_heads * HEAD_DIM), BF16),
        scratch_shapes=[pltpu.VMEM((hp, HEAD_DIM, HEAD_DIM), F32)],
        compiler_params=_params("arbitrary", "arbitrary"),
        name="hgrn2",
    )(proj, proj, proj, proj, lb_logits, norm_g.reshape(1, HEAD_DIM), tri, m_lev, level)


def _rglru_kernel(gate_ref, x_ref, cw_ref, cb_ref, wax_ref, bax_ref, lam_ref, o_ref, *, chunk, n_chunks):
    w = RG_BLOCK_WIDTH
    cw = cw_ref[...]
    cb = cb_ref[...]
    lam = lam_ref[...]
    neg_c_softplus = -RG_C * (jnp.maximum(-lam, 0.0) + jnp.log1p(jnp.exp(-jnp.abs(lam))))
    row = lax.broadcasted_iota(jnp.int32, (chunk, w), 0)
    sub = row & (SUBLANES - 1)

    def body(ci, carry):
        x_prev, h_prev = carry
        r0 = pl.multiple_of(ci * chunk, chunk)
        rows = pl.ds(r0, chunk)
        xc = x_ref[0, rows, :].astype(F32)
        xh = jnp.concatenate([x_prev, xc], axis=0)
        y = cb + xc * cw[0:1, :]
        for tap in range(1, CONV_WIDTH):
            y = y + pltpu.roll(xh, tap, axis=0)[SUBLANES:, :] * cw[tap:tap + 1, :]
        ax = _dot(y.astype(BF16), wax_ref[0]) + bax_ref[0]
        r = _sigmoid(ax[:, :w])
        gi = _sigmoid(ax[:, w:])
        log_a = r * neg_c_softplus
        a = jnp.exp(log_a)
        mult = jnp.sqrt(-jnp.tanh(log_a) * (a * a + 1.0))
        mult = jnp.where(row + r0 == 0, 1.0, mult)
        u = y * gi * mult
        for shift in (1, 2, 4):
            keep = sub >= shift
            a_sh = pltpu.roll(a, shift, axis=0)
            u_sh = pltpu.roll(u, shift, axis=0)
            u = jnp.where(keep, u + a * u_sh, u)
            a = jnp.where(keep, a * a_sh, a)
        hs = []
        h_last = h_prev
        for gidx in range(chunk // SUBLANES):
            sl = slice(gidx * SUBLANES, (gidx + 1) * SUBLANES)
            hg = u[sl] + a[sl] * h_last
            hs.append(hg)
            h_last = hg[SUBLANES - 1:SUBLANES, :]
        h = jnp.concatenate(hs, axis=0)
        gt = gate_ref[0, rows, :].astype(F32)
        gelu = 0.5 * gt * (1.0 + jnp.tanh(np.sqrt(2.0 / np.pi) * (gt + 0.044715 * (gt * gt * gt))))
        o_ref[0, rows, :] = (gelu * h).astype(o_ref.dtype)
        return xc[chunk - SUBLANES:, :], h_last

    init = (jnp.zeros((SUBLANES, w), F32), jnp.zeros((1, w), F32))
    lax.fori_loop(0, n_chunks, body, init)


def _rglru(proj, conv_w, conv_b, wa, ba, wx, bx, lam, *, chunk_target=256):
    b, s, w2 = proj.shape
    lru = w2 // 2
    w = RG_BLOCK_WIDTH
    nb = lru // w
    chunk = _tile(s, chunk_target, 2 * SUBLANES)
    wax = jnp.concatenate([wa, wx], axis=-1).astype(BF16)
    bax = jnp.concatenate([ba, bx], axis=-1).reshape(nb, 1, 2 * w)
    kern = functools.partial(_rglru_kernel, chunk=chunk, n_chunks=s // chunk)
    return pl.pallas_call(
        kern,
        grid=(b, nb),
        in_specs=[
            pl.BlockSpec((1, s, w), lambda bi, ni: (bi, 0, ni)),
            pl.BlockSpec((1, s, w), lambda bi, ni: (bi, 0, nb + ni)),
            pl.BlockSpec((CONV_WIDTH, w), lambda bi, ni: (0, ni)),
            pl.BlockSpec((1, w), lambda bi, ni: (0, ni)),
            pl.BlockSpec((1, w, 2 * w), lambda bi, ni: (ni, 0, 0)),
            pl.BlockSpec((1, 1, 2 * w), lambda bi, ni: (ni, 0, 0)),
            pl.BlockSpec((1, w), lambda bi, ni: (0, ni)),
        ],
        out_specs=pl.BlockSpec((1, s, w), lambda bi, ni: (bi, 0, ni)),
        out_shape=jax.ShapeDtypeStruct((b, s, lru), BF16),
        compiler_params=_params("arbitrary", "arbitrary"),
        name="rglru",
    )(proj, proj, conv_w, conv_b.reshape(1, lru), wax, bax, lam.reshape(1, lru))


def kernel(x, p, mix_pre_g, mix_post_g, ffn_pre_g, ffn_post_g, ple_norm_g, w_in_even, w_out_even, hg_lb_logits, hg_norm_g, w_in_odd, conv_w, conv_b, rg_wa, rg_ba, rg_wx, rg_bx, rg_lambda, w_out_odd, w_gate_up, w_down, w_ple_up, w_ple_gate):
    b, s, d = x.shape
    t = b * s
    depth = p.shape[0]
    h = x.reshape(t, d)
    for i in range(depth):
        j = i // 2
        if i % 2 == 0:
            n_heads = w_out_even.shape[1] // (2 * HEAD_DIM)
            proj = _norm_matmul(h, mix_pre_g[i], w_in_even, j, BF16)
            proj = proj.reshape(b, s, -1)
            a_out = _sb_attention(proj, n_heads)
            b_out = _hgrn2(proj, hg_lb_logits, hg_norm_g[j], n_heads, 3 * n_heads, j)
            mixed = jnp.concatenate([a_out, b_out], axis=-1).reshape(t, -1)
            w_out = w_out_even[j].astype(BF16)
        else:
            proj = _norm_matmul(h, mix_pre_g[i], w_in_odd, j, BF16)
            mixed = _rglru(proj.reshape(b, s, -1), conv_w[j], conv_b[j], rg_wa[j], rg_ba[j],
                           rg_wx[j], rg_bx[j], rg_lambda[j]).reshape(t, -1)
            w_out = w_out_odd[j].astype(BF16)
        h = _matmul_norm_residual(mixed, w_out, mix_post_g[i], h, tk_target=2048)
        act = _norm_swiglu(h, ffn_pre_g[i], w_gate_up, i)
        h = _matmul_norm_residual(act, w_down[i].astype(BF16), ffn_post_g[i], h)
        h = _per_layer_embedding(h, p.reshape(depth, t, -1), i, w_ple_up[i].astype(BF16),
                                 w_ple_gate[i].astype(BF16), ple_norm_g[i])
    return h.reshape(b, s, d)
```

```python
import functools

import numpy as np
import jax
import jax.numpy as jnp
from jax import lax
from jax.experimental import pallas as pl
from jax.experimental.pallas import tpu as pltpu

F32 = jnp.float32
BF16 = jnp.bfloat16

RMS_EPS = 1e-6
LOG2_E = float(np.log2(np.e))
HEAD_DIM = 128
RG_BLOCK_WIDTH = 256
CONV_WIDTH = 4
RG_C = 8.0
V7X_VMEM_BYTES = 64 * 1024 * 1024
VMEM_LIMIT_BYTES = V7X_VMEM_BYTES - 8 * 1024 * 1024
SUBLANES = 8
HG_BLOCK = 128
HG_LEVELS = 7
HG_SAFE_EXPONENT = 80.0
SB_DEAD_LOG2 = -160.0


def _tile(n, target, quantum):
    if n <= target:
        return n
    t = (target // quantum) * quantum
    while t > quantum and n % t:
        t -= quantum
    assert n % t == 0, (n, target, quantum)
    return t


def _params(*semantics):
    return pltpu.CompilerParams(dimension_semantics=semantics, vmem_limit_bytes=VMEM_LIMIT_BYTES)


def _sigmoid(x):
    return 0.5 + 0.5 * jnp.tanh(0.5 * x)


def _rms_scale(x):
    return lax.rsqrt(jnp.mean(x * x, axis=-1, keepdims=True) + RMS_EPS)


def _dot(a, b):
    return jnp.dot(a, b, preferred_element_type=F32)


def _dot_nt(a, b):
    return lax.dot_general(a, b, (((1,), (1,)), ((), ())), preferred_element_type=F32)


def _split_bf16(x):
    hi = x.astype(BF16)
    lo = (x - hi.astype(F32)).astype(BF16)
    return hi, lo


def _norm_matmul_kernel(x_ref, g_ref, w_ref, o_ref, xn_ref):
    @pl.when(pl.program_id(1) == 0)
    def _():
        x = x_ref[...]
        xn_ref[...] = (x * _rms_scale(x) * g_ref[...]).astype(BF16)

    o_ref[...] = _dot(xn_ref[...], w_ref[...].astype(BF16)).astype(o_ref.dtype)


def _norm_swiglu_kernel(x_ref, g_ref, wg_ref, wu_ref, o_ref, xn_ref):
    @pl.when(pl.program_id(1) == 0)
    def _():
        x = x_ref[...]
        xn_ref[...] = (x * _rms_scale(x) * g_ref[...]).astype(BF16)

    xn = xn_ref[...]
    gate = _dot(xn, wg_ref[...].astype(BF16))
    up = _dot(xn, wu_ref[...].astype(BF16))
    o_ref[...] = (gate * _sigmoid(gate) * up).astype(o_ref.dtype)


def _norm_matmul(x, g, w, layer, out_dtype, *, tm_target=1024, tn_target=512):
    t, d = x.shape
    n = w.shape[2]
    tm = _tile(t, tm_target, SUBLANES)
    tn = _tile(n, tn_target, 128)
    return pl.pallas_call(
        _norm_matmul_kernel,
        grid=(t // tm, n // tn),
        in_specs=[
            pl.BlockSpec((tm, d), lambda i, j: (i, 0)),
            pl.BlockSpec((1, d), lambda i, j: (0, 0)),
            pl.BlockSpec((None, d, tn), lambda i, j: (layer, 0, j)),
        ],
        out_specs=pl.BlockSpec((tm, tn), lambda i, j: (i, j)),
        out_shape=jax.ShapeDtypeStruct((t, n), out_dtype),
        scratch_shapes=[pltpu.VMEM((tm, d), BF16)],
        compiler_params=_params("arbitrary", "arbitrary"),
        name="norm_matmul",
    )(x, g.reshape(1, d), w)


def _norm_swiglu(x, g, w_gate_up, layer, *, tm_target=1024, tn_target=512):
    t, d = x.shape
    d_ff = w_gate_up.shape[2] // 2
    tm = _tile(t, tm_target, SUBLANES)
    tn = _tile(d_ff, tn_target, 128)
    nj = d_ff // tn
    return pl.pallas_call(
        _norm_swiglu_kernel,
        grid=(t // tm, nj),
        in_specs=[
            pl.BlockSpec((tm, d), lambda i, j: (i, 0)),
            pl.BlockSpec((1, d), lambda i, j: (0, 0)),
            pl.BlockSpec((None, d, tn), lambda i, j: (layer, 0, j)),
            pl.BlockSpec((None, d, tn), lambda i, j: (layer, 0, j + nj)),
        ],
        out_specs=pl.BlockSpec((tm, tn), lambda i, j: (i, j)),
        out_shape=jax.ShapeDtypeStruct((t, d_ff), BF16),
        scratch_shapes=[pltpu.VMEM((tm, d), BF16)],
        compiler_params=_params("arbitrary", "arbitrary"),
        name="norm_swiglu",
    )(x, g.reshape(1, d), w_gate_up, w_gate_up)


def _matmul_norm_residual_kernel(x_ref, w_ref, g_ref, h_ref, o_ref, acc_ref, *, nk):
    k = pl.program_id(1)

    def finish(m):
        o_ref[...] = h_ref[...] + m * _rms_scale(m) * g_ref[...]

    if nk == 1:
        finish(_dot(x_ref[...], w_ref[...]))
        return

    @pl.when(k == 0)
    def _():
        acc_ref[...] = jnp.zeros_like(acc_ref)

    acc_ref[...] += _dot(x_ref[...], w_ref[...])

    @pl.when(k == nk - 1)
    def _():
        finish(acc_ref[...])


def _matmul_norm_residual(x, w, g, h, *, tm_target=512, tk_target=1408):
    t, kdim = x.shape
    d = w.shape[1]
    tm = _tile(t, tm_target, SUBLANES)
    tk = _tile(kdim, tk_target, 128)
    nk = kdim // tk
    return pl.pallas_call(
        functools.partial(_matmul_norm_residual_kernel, nk=nk),
        grid=(t // tm, nk),
        in_specs=[
            pl.BlockSpec((tm, tk), lambda i, k: (i, k)),
            pl.BlockSpec((tk, d), lambda i, k: (k, 0)),
            pl.BlockSpec((1, d), lambda i, k: (0, 0)),
            pl.BlockSpec((tm, d), lambda i, k: (i, 0)),
        ],
        out_specs=pl.BlockSpec((tm, d), lambda i, k: (i, 0)),
        out_shape=jax.ShapeDtypeStruct((t, d), F32),
        scratch_shapes=[pltpu.VMEM((tm, d), F32)],
        compiler_params=_params("arbitrary", "arbitrary"),
        name="matmul_norm_residual",
    )(x, w, g.reshape(1, d), h)


def _ple_kernel(h_ref, p_ref, wup_ref, wg_ref, g_ref, o_ref):
    h = h_ref[...]
    e = _dot(p_ref[...].astype(BF16), wup_ref[...])
    gate = _sigmoid(_dot(h.astype(BF16), wg_ref[...]))
    y = gate * e
    o_ref[...] = h + y * _rms_scale(y) * g_ref[...]


def _per_layer_embedding(h, p, layer, w_up, w_gate, g, *, tm_target=512):
    t, d = h.shape
    pd = p.shape[2]
    tm = _tile(t, tm_target, SUBLANES)
    return pl.pallas_call(
        _ple_kernel,
        grid=(t // tm,),
        in_specs=[
            pl.BlockSpec((tm, d), lambda i: (i, 0)),
            pl.BlockSpec((None, tm, pd), lambda i: (layer, i, 0)),
            pl.BlockSpec((pd, d), lambda i: (0, 0)),
            pl.BlockSpec((d, d), lambda i: (0, 0)),
            pl.BlockSpec((1, d), lambda i: (0, 0)),
        ],
        out_specs=pl.BlockSpec((tm, d), lambda i: (i, 0)),
        out_shape=jax.ShapeDtypeStruct((t, d), F32),
        compiler_params=_params("arbitrary"),
        name="per_layer_embedding",
    )(h, p, w_up, w_gate, g.reshape(1, d))


def _sb_attention_kernel(q_ref, k_ref, v_ref, u_ref, o_ref, *, tile, heads, scale):
    i = pl.program_id(2)
    u = u_ref[...]
    qs = [(q_ref[0, :, h * HEAD_DIM:(h + 1) * HEAD_DIM].astype(F32) * scale).astype(BF16)
          for h in range(heads)]
    t_idx = lax.broadcasted_iota(jnp.int32, (tile, tile), 0)
    s_idx = lax.broadcasted_iota(jnp.int32, (tile, tile), 1)
    below_diag = s_idx < t_idx

    def key_tile(j, state, masked):
        start = pl.multiple_of(j * tile, tile)
        hs = range(heads)
        cols = [slice(h * HEAD_DIM, (h + 1) * HEAD_DIM) for h in hs]
        z = [_dot_nt(qs[h], k_ref[0, pl.ds(start, tile), cols[h]]) for h in hs]
        sp = [jnp.maximum(z[h], 0.0) + jnp.log(1.0 + jnp.exp2(-jnp.abs(z[h]))) * LOG2_E for h in hs]
        if masked:
            sp = [jnp.where(below_diag, sp[h], 0.0) for h in hs]
        split = [jnp.concatenate(_split_bf16(sp[h]), axis=1) for h in hs]
        log_rem = [_dot(split[h], u) for h in hs]
        w = [jnp.exp2((z[h] - sp[h]) + log_rem[h] + state[h][1]) for h in hs]
        if masked:
            w = [jnp.where(below_diag, w[h], 0.0) for h in hs]
        pv = [_dot(w[h].astype(BF16), v_ref[0, pl.ds(start, tile), cols[h]]) for h in hs]
        return tuple((state[h][0] + pv[h], state[h][1] - jnp.sum(sp[h], axis=1, keepdims=True)) for h in hs)

    def any_weight_left(state):
        top = functools.reduce(jnp.maximum, [state[h][1] for h in range(heads)])
        return jnp.max(top) > SB_DEAD_LOG2

    state = tuple((jnp.zeros((tile, HEAD_DIM), F32), jnp.zeros((tile, 1), F32)) for _ in range(heads))
    state = key_tile(i, state, True)

    def cond(carry):
        return jnp.logical_and(carry[0] < i, carry[1])

    def body(carry):
        new_state = key_tile(i - 1 - carry[0], carry[2], False)
        return carry[0] + 1, any_weight_left(new_state), new_state

    state = lax.while_loop(cond, body, (jnp.int32(0), any_weight_left(state), state))[2]
    for h in range(heads):
        o_ref[0, :, h * HEAD_DIM:(h + 1) * HEAD_DIM] = state[h][0].astype(o_ref.dtype)


def _sb_attention(proj, n_heads, *, tile_target=256, heads_per_step=4):
    b, s, _ = proj.shape
    tile = _tile(s, tile_target, 128)
    hp = heads_per_step if n_heads % heads_per_step == 0 else 1
    groups = n_heads // hp
    rows = np.arange(tile)
    neg_lower = -(rows[:, None] > rows[None, :]).astype(np.float32)
    u = jnp.asarray(np.concatenate([neg_lower, neg_lower], axis=0), BF16)
    kern = functools.partial(_sb_attention_kernel, tile=tile, heads=hp, scale=HEAD_DIM ** -0.5 * LOG2_E)
    width = hp * HEAD_DIM
    return pl.pallas_call(
        kern,
        grid=(b, groups, s // tile),
        in_specs=[
            pl.BlockSpec((1, tile, width), lambda bi, gi, qi: (bi, qi, gi)),
            pl.BlockSpec((1, s, width), lambda bi, gi, qi: (bi, 0, groups + gi)),
            pl.BlockSpec((1, s, width), lambda bi, gi, qi: (bi, 0, 2 * groups + gi)),
            pl.BlockSpec((2 * tile, tile), lambda bi, gi, qi: (0, 0)),
        ],
        out_specs=pl.BlockSpec((1, tile, width), lambda bi, gi, qi: (bi, qi, gi)),
        out_shape=jax.ShapeDtypeStruct((b, s, n_heads * HEAD_DIM), BF16),
        compiler_params=_params("arbitrary", "arbitrary", "arbitrary"),
        name="sb_attention",
    )(proj, proj, proj, u)


def _hgrn2_constants():
    n = HG_BLOCK
    r = np.arange(n)
    tri = (r[None, :] <= r[:, None])
    mats = [r[None, :] > r[:, None]]
    level = np.where(r[:, None] == r[None, :], 0, -1).astype(np.int32)
    for lv in range(1, HG_LEVELS + 1):
        half = n >> lv
        pos = r % (2 * half)
        bnd = r - pos + half - 1
        upper = pos >= half
        m_q = (r[None, :] > bnd[:, None]) & (r[None, :] <= r[:, None]) & upper[:, None]
        m_k = (r[None, :] > r[:, None]) & (r[None, :] <= bnd[:, None]) & ~upper[:, None]
        mats.append(m_q | m_k)
        same_group = (r[:, None] // (2 * half)) == (r[None, :] // (2 * half))
        pair = same_group & upper[:, None] & ~upper[None, :]
        level = np.where(pair, lv, level)
    m_lev = np.concatenate(mats, axis=0).astype(np.float32)
    return jnp.asarray(tri.astype(np.float32), BF16), jnp.asarray(m_lev, BF16), jnp.asarray(level)


def _hgrn2_kernel(hq_ref, hf_ref, hi_ref, hg_ref, lbl_ref, ng_ref, tri_ref, mlev_ref, lv_ref, o_ref, st_ref,
                  *, n_blocks, layer_j, heads):
    n = HG_BLOCK
    mid_row = n // 2 - 1
    logits = lbl_ref[...]
    e = jnp.exp(logits - jnp.max(logits, axis=0, keepdims=True))
    sm = e / jnp.sum(e, axis=0, keepdims=True)
    lb_all = jnp.sum(sm[: layer_j + 1], axis=0, keepdims=True)
    ng = ng_ref[...]
    st_ref[...] = jnp.zeros_like(st_ref)

    def block(bi, carry):
        r0 = pl.multiple_of(bi * n, n)
        rows = pl.ds(r0, n)
        level = lv_ref[...]

        hs = range(heads)
        col_of = [slice(h * HEAD_DIM, (h + 1) * HEAD_DIM) for h in hs]
        f = [lb_all[:, col_of[h]] + (1.0 - lb_all[:, col_of[h]]) * _sigmoid(hf_ref[0, rows, col_of[h]].astype(F32))
             for h in hs]
        g2 = [jnp.concatenate(_split_bf16(jnp.log(f[h])), axis=1) for h in hs]
        c2 = [_dot(tri_ref[...], g2[h]) for h in hs]
        cum = [c2[h][:, :HEAD_DIM] + c2[h][:, HEAD_DIM:] for h in hs]
        mid = [cum[h][mid_row:mid_row + 1, :] for h in hs]
        last = [cum[h][n - 1:n, :] for h in hs]
        worst = functools.reduce(jnp.minimum, [jnp.minimum(mid[h], last[h] - mid[h]) for h in hs])
        safe = jnp.min(worst) >= -HG_SAFE_EXPONENT
        kk = [1.0 - f[h] for h in hs]
        hq = [hq_ref[0, rows, col_of[h]].astype(F32) for h in hs]
        q = [hq[h] * _sigmoid(hq[h]) for h in hs]
        pre = [(col_of[h], q[h], kk[h], g2[h], cum[h], mid[h], last[h]) for h in hs]

        def finish(h, cols, o):
            hg = hg_ref[0, rows, cols].astype(F32)
            out = o * _rms_scale(o) * ng * (hg * _sigmoid(hg))
            o_ref[0, rows, cols] = out.astype(o_ref.dtype)

        @pl.when(safe)
        def _():
            v = [hi_ref[0, rows, col_of[h]] for h in hs]
            qd = [(q[h] * jnp.exp(cum[h] - mid[h])).astype(BF16) for h in hs]
            kd = [kk[h] * jnp.exp(mid[h] - cum[h]) for h in hs]
            st = [st_ref[h] for h in hs]
            o = [_dot_nt(qd[h], (st[h] * jnp.exp(mid[h])).astype(BF16)) for h in hs]
            sc = [_dot_nt(qd[h], kd[h].astype(BF16)) for h in hs]
            p = [jnp.where(level >= 0, sc[h], 0.0).astype(BF16) for h in hs]
            o = [o[h] + _dot(p[h], v[h]) for h in hs]
            k_dec = [(kd[h] * jnp.exp(last[h] - mid[h])).astype(BF16) for h in hs]
            vt = [v[h].astype(F32).T.astype(BF16) for h in hs]
            for h in hs:
                st_ref[h] = st[h] * jnp.exp(last[h]) + _dot(vt[h], k_dec[h])
                finish(h, col_of[h], o[h])

        @pl.when(jnp.logical_not(safe))
        def _():
            for h, (cols, q, kk, g2, cum, mid, last) in enumerate(pre):
                v = hi_ref[0, rows, cols]
                e2 = _dot(mlev_ref[...], g2)
                x = jnp.exp(e2[:, :HEAD_DIM] + e2[:, HEAD_DIM:])
                st = st_ref[h]
                o = _dot_nt((q * jnp.exp(cum)).astype(BF16), st.astype(BF16))
                p = jnp.where(level == 0, _dot_nt(q.astype(BF16), kk.astype(BF16)), 0.0)
                for lv in range(1, HG_LEVELS + 1):
                    xl = x[lv * n:(lv + 1) * n]
                    sc = _dot_nt((q * xl).astype(BF16), (kk * xl).astype(BF16))
                    p = jnp.where(level == lv, sc, p)
                o = o + _dot(p.astype(BF16), v)
                k_dec = (kk * x[0:n]).astype(BF16)
                vt = v.astype(F32).T.astype(BF16)
                st_ref[h] = st * jnp.exp(last) + _dot(vt, k_dec)
                finish(h, cols, o)

        return carry

    lax.fori_loop(0, n_blocks, block, 0)


def _hgrn2(proj, lb_logits, norm_g, n_heads, col0, layer_j, *, heads_per_step=4):
    b, s, _ = proj.shape
    assert s % HG_BLOCK == 0
    hp = heads_per_step if (n_heads % heads_per_step == 0 and col0 % heads_per_step == 0) else 1
    groups = n_heads // hp
    width = hp * HEAD_DIM
    tri, m_lev, level = _hgrn2_constants()
    n_rows = lb_logits.shape[0]

    def col(off):
        return pl.BlockSpec((1, s, width), lambda bi, gi: (bi, 0, col0 // hp + off * groups + gi))

    kern = functools.partial(_hgrn2_kernel, n_blocks=s // HG_BLOCK, layer_j=layer_j, heads=hp)
    return pl.pallas_call(
        kern,
        grid=(b, groups),
        in_specs=[
            col(0), col(1), col(2), col(3),
            pl.BlockSpec((n_rows, width), lambda bi, gi: (0, gi)),
            pl.BlockSpec((1, HEAD_DIM), lambda bi, gi: (0, 0)),
            pl.BlockSpec(tri.shape, lambda bi, gi: (0, 0)),
            pl.BlockSpec(m_lev.shape, lambda bi, gi: (0, 0)),
            pl.BlockSpec(level.shape, lambda bi, gi: (0, 0)),
        ],
        out_specs=pl.BlockSpec((1, s, width), lambda bi, gi: (bi, 0, gi)),
        out_shape=jax.ShapeDtypeStruct((b, s, n_heads * HEAD_DIM), BF16),
        scratch_shapes=[pltpu.VMEM((hp, HEAD_DIM, HEAD_DIM), F32)],
        compiler_params=_params("arbitrary", "arbitrary"),
        name="hgrn2",
    )(proj, proj, proj, proj, lb_logits, norm_g.reshape(1, HEAD_DIM), tri, m_lev, level)


def _rglru_kernel(gate_ref, x_ref, cw_ref, cb_ref, wax_ref, bax_ref, lam_ref, o_ref, *, chunk, n_chunks):
    w = RG_BLOCK_WIDTH
    cw = cw_ref[...]
    cb = cb_ref[...]
    lam = lam_ref[...]
    neg_c_softplus = -RG_C * (jnp.maximum(-lam, 0.0) + jnp.log1p(jnp.exp(-jnp.abs(lam))))
    row = lax.broadcasted_iota(jnp.int32, (chunk, w), 0)
    sub = row & (SUBLANES - 1)

    def body(ci, carry):
        x_prev, h_prev = carry
        r0 = pl.multiple_of(ci * chunk, chunk)
        rows = pl.ds(r0, chunk)
        xc = x_ref[0, rows, :].astype(F32)
        xh = jnp.concatenate([x_prev, xc], axis=0)
        y = cb + xc * cw[0:1, :]
        for tap in range(1, CONV_WIDTH):
            y = y + pltpu.roll(xh, tap, axis=0)[SUBLANES:, :] * cw[tap:tap + 1, :]
        ax = _dot(y.astype(BF16), wax_ref[0]) + bax_ref[0]
        r = _sigmoid(ax[:, :w])
        gi = _sigmoid(ax[:, w:])
        log_a = r * neg_c_softplus
        a = jnp.exp(log_a)
        mult = jnp.sqrt(-jnp.tanh(log_a) * (a * a + 1.0))
        mult = jnp.where(row + r0 == 0, 1.0, mult)
        u = y * gi * mult
        for shift in (1, 2, 4):
            keep = sub >= shift
            a_sh = pltpu.roll(a, shift, axis=0)
            u_sh = pltpu.roll(u, shift, axis=0)
            u = jnp.where(keep, u + a * u_sh, u)
            a = jnp.where(keep, a * a_sh, a)
        hs = []
        h_last = h_prev
        for gidx in range(chunk // SUBLANES):
            sl = slice(gidx * SUBLANES, (gidx + 1) * SUBLANES)
            hg = u[sl] + a[sl] * h_last
            hs.append(hg)
            h_last = hg[SUBLANES - 1:SUBLANES, :]
        h = jnp.concatenate(hs, axis=0)
        gt = gate_ref[0, rows, :].astype(F32)
        gelu = 0.5 * gt * (1.0 + jnp.tanh(np.sqrt(2.0 / np.pi) * (gt + 0.044715 * (gt * gt * gt))))
        o_ref[0, rows, :] = (gelu * h).astype(o_ref.dtype)
        return xc[chunk - SUBLANES:, :], h_last

    init = (jnp.zeros((SUBLANES, w), F32), jnp.zeros((1, w), F32))
    lax.fori_loop(0, n_chunks, body, init)


def _rglru(proj, conv_w, conv_b, wa, ba, wx, bx, lam, *, chunk_target=256):
    b, s, w2 = proj.shape
    lru = w2 // 2
    w = RG_BLOCK_WIDTH
    nb = lru // w
    chunk = _tile(s, chunk_target, 2 * SUBLANES)
    wax = jnp.concatenate([wa, wx], axis=-1).astype(BF16)
    bax = jnp.concatenate([ba, bx], axis=-1).reshape(nb, 1, 2 * w)
    kern = functools.partial(_rglru_kernel, chunk=chunk, n_chunks=s // chunk)
    return pl.pallas_call(
        kern,
        grid=(b, nb),
        in_specs=[
            pl.BlockSpec((1, s, w), lambda bi, ni: (bi, 0, ni)),
            pl.BlockSpec((1, s, w), lambda bi, ni: (bi, 0, nb + ni)),
            pl.BlockSpec((CONV_WIDTH, w), lambda bi, ni: (0, ni)),
            pl.BlockSpec((1, w), lambda bi, ni: (0, ni)),
            pl.BlockSpec((1, w, 2 * w), lambda bi, ni: (ni, 0, 0)),
            pl.BlockSpec((1, 1, 2 * w), lambda bi, ni: (ni, 0, 0)),
            pl.BlockSpec((1, w), lambda bi, ni: (0, ni)),
        ],
        out_specs=pl.BlockSpec((1, s, w), lambda bi, ni: (bi, 0, ni)),
        out_shape=jax.ShapeDtypeStruct((b, s, lru), BF16),
        compiler_params=_params("arbitrary", "arbitrary"),
        name="rglru",
    )(proj, proj, conv_w, conv_b.reshape(1, lru), wax, bax, lam.reshape(1, lru))


def kernel(x, p, mix_pre_g, mix_post_g, ffn_pre_g, ffn_post_g, ple_norm_g, w_in_even, w_out_even, hg_lb_logits, hg_norm_g, w_in_odd, conv_w, conv_b, rg_wa, rg_ba, rg_wx, rg_bx, rg_lambda, w_out_odd, w_gate_up, w_down, w_ple_up, w_ple_gate):
    b, s, d = x.shape
    t = b * s
    depth = p.shape[0]
    h = x.reshape(t, d)
    for i in range(depth):
        j = i // 2
        if i % 2 == 0:
            n_heads = w_out_even.shape[1] // (2 * HEAD_DIM)
            proj = _norm_matmul(h, mix_pre_g[i], w_in_even, j, BF16, tn_target=1024)
            proj = proj.reshape(b, s, -1)
            a_out = _sb_attention(proj, n_heads)
            b_out = _hgrn2(proj, hg_lb_logits, hg_norm_g[j], n_heads, 3 * n_heads, j)
            mixed = jnp.concatenate([a_out, b_out], axis=-1).reshape(t, -1)
            w_out = w_out_even[j].astype(BF16)
        else:
            proj = _norm_matmul(h, mix_pre_g[i], w_in_odd, j, BF16)
            mixed = _rglru(proj.reshape(b, s, -1), conv_w[j], conv_b[j], rg_wa[j], rg_ba[j],
                           rg_wx[j], rg_bx[j], rg_lambda[j]).reshape(t, -1)
            w_out = w_out_odd[j].astype(BF16)
        h = _matmul_norm_residual(mixed, w_out, mix_post_g[i], h, tk_target=2048)
        act = _norm_swiglu(h, ffn_pre_g[i], w_gate_up, i)
        h = _matmul_norm_residual(act, w_down[i].astype(BF16), ffn_post_g[i], h)
        h = _per_layer_embedding(h, p.reshape(depth, t, -1), i, w_ple_up[i].astype(BF16),
                                 w_ple_gate[i].astype(BF16), ple_norm_g[i])
    return h.reshape(b, s, d)
```

```python
import functools

import numpy as np
import jax
import jax.numpy as jnp
from jax import lax
from jax.experimental import pallas as pl
from jax.experimental.pallas import tpu as pltpu

F32 = jnp.float32
BF16 = jnp.bfloat16

RMS_EPS = 1e-6
LOG2_E = float(np.log2(np.e))
HEAD_DIM = 128
RG_BLOCK_WIDTH = 256
CONV_WIDTH = 4
RG_C = 8.0
V7X_VMEM_BYTES = 64 * 1024 * 1024
VMEM_LIMIT_BYTES = V7X_VMEM_BYTES - 8 * 1024 * 1024
SUBLANES = 8
HG_BLOCK = 128
HG_LEVELS = 7
HG_SAFE_EXPONENT = 80.0
SB_DEAD_LOG2 = -160.0


def _tile(n, target, quantum):
    if n <= target:
        return n
    t = (target // quantum) * quantum
    while t > quantum and n % t:
        t -= quantum
    assert n % t == 0, (n, target, quantum)
    return t


def _params(*semantics):
    return pltpu.CompilerParams(dimension_semantics=semantics, vmem_limit_bytes=VMEM_LIMIT_BYTES)


def _sigmoid(x):
    return 0.5 + 0.5 * jnp.tanh(0.5 * x)


def _rms_scale(x):
    return lax.rsqrt(jnp.mean(x * x, axis=-1, keepdims=True) + RMS_EPS)


def _dot(a, b):
    return jnp.dot(a, b, preferred_element_type=F32)


def _dot_nt(a, b):
    return lax.dot_general(a, b, (((1,), (1,)), ((), ())), preferred_element_type=F32)


def _split_bf16(x):
    hi = x.astype(BF16)
    lo = (x - hi.astype(F32)).astype(BF16)
    return hi, lo


def _norm_matmul_kernel(x_ref, g_ref, w_ref, o_ref, xn_ref):
    @pl.when(pl.program_id(1) == 0)
    def _():
        x = x_ref[...]
        xn_ref[...] = (x * _rms_scale(x) * g_ref[...]).astype(BF16)

    o_ref[...] = _dot(xn_ref[...], w_ref[...].astype(BF16)).astype(o_ref.dtype)


def _norm_swiglu_kernel(x_ref, g_ref, wg_ref, wu_ref, o_ref, xn_ref):
    @pl.when(pl.program_id(1) == 0)
    def _():
        x = x_ref[...]
        xn_ref[...] = (x * _rms_scale(x) * g_ref[...]).astype(BF16)

    xn = xn_ref[...]
    gate = _dot(xn, wg_ref[...].astype(BF16))
    up = _dot(xn, wu_ref[...].astype(BF16))
    o_ref[...] = (gate * _sigmoid(gate) * up).astype(o_ref.dtype)


def _norm_matmul(x, g, w, layer, out_dtype, *, tm_target=1024, tn_target=512):
    t, d = x.shape
    n = w.shape[2]
    tm = _tile(t, tm_target, SUBLANES)
    tn = _tile(n, tn_target, 128)
    return pl.pallas_call(
        _norm_matmul_kernel,
        grid=(t // tm, n // tn),
        in_specs=[
            pl.BlockSpec((tm, d), lambda i, j: (i, 0)),
            pl.BlockSpec((1, d), lambda i, j: (0, 0)),
            pl.BlockSpec((None, d, tn), lambda i, j: (layer, 0, j)),
        ],
        out_specs=pl.BlockSpec((tm, tn), lambda i, j: (i, j)),
        out_shape=jax.ShapeDtypeStruct((t, n), out_dtype),
        scratch_shapes=[pltpu.VMEM((tm, d), BF16)],
        compiler_params=_params("arbitrary", "arbitrary"),
        name="norm_matmul",
    )(x, g.reshape(1, d), w)


def _norm_swiglu(x, g, w_gate_up, layer, *, tm_target=1024, tn_target=512):
    t, d = x.shape
    d_ff = w_gate_up.shape[2] // 2
    tm = _tile(t, tm_target, SUBLANES)
    tn = _tile(d_ff, tn_target, 128)
    nj = d_ff // tn
    return pl.pallas_call(
        _norm_swiglu_kernel,
        grid=(t // tm, nj),
        in_specs=[
            pl.BlockSpec((tm, d), lambda i, j: (i, 0)),
            pl.BlockSpec((1, d), lambda i, j: (0, 0)),
            pl.BlockSpec((None, d, tn), lambda i, j: (layer, 0, j)),
            pl.BlockSpec((None, d, tn), lambda i, j: (layer, 0, j + nj)),
        ],
        out_specs=pl.BlockSpec((tm, tn), lambda i, j: (i, j)),
        out_shape=jax.ShapeDtypeStruct((t, d_ff), BF16),
        scratch_shapes=[pltpu.VMEM((tm, d), BF16)],
        compiler_params=_params("arbitrary", "arbitrary"),
        name="norm_swiglu",
    )(x, g.reshape(1, d), w_gate_up, w_gate_up)


def _matmul_norm_residual_kernel(x_ref, w_ref, g_ref, h_ref, o_ref, acc_ref, *, nk):
    k = pl.program_id(1)

    def finish(m):
        o_ref[...] = h_ref[...] + m * _rms_scale(m) * g_ref[...]

    if nk == 1:
        finish(_dot(x_ref[...], w_ref[...]))
        return

    @pl.when(k == 0)
    def _():
        acc_ref[...] = jnp.zeros_like(acc_ref)

    acc_ref[...] += _dot(x_ref[...], w_ref[...])

    @pl.when(k == nk - 1)
    def _():
        finish(acc_ref[...])


def _matmul_norm_residual(x, w, g, h, *, tm_target=512, tk_target=1408):
    t, kdim = x.shape
    d = w.shape[1]
    tm = _tile(t, tm_target, SUBLANES)
    tk = _tile(kdim, tk_target, 128)
    nk = kdim // tk
    return pl.pallas_call(
        functools.partial(_matmul_norm_residual_kernel, nk=nk),
        grid=(t // tm, nk),
        in_specs=[
            pl.BlockSpec((tm, tk), lambda i, k: (i, k)),
            pl.BlockSpec((tk, d), lambda i, k: (k, 0)),
            pl.BlockSpec((1, d), lambda i, k: (0, 0)),
            pl.BlockSpec((tm, d), lambda i, k: (i, 0)),
        ],
        out_specs=pl.BlockSpec((tm, d), lambda i, k: (i, 0)),
        out_shape=jax.ShapeDtypeStruct((t, d), F32),
        scratch_shapes=[pltpu.VMEM((tm, d), F32)],
        compiler_params=_params("arbitrary", "arbitrary"),
        name="matmul_norm_residual",
    )(x, w, g.reshape(1, d), h)


def _ple_kernel(h_ref, p_ref, wup_ref, wg_ref, g_ref, o_ref):
    h = h_ref[...]
    e = _dot(p_ref[...].astype(BF16), wup_ref[...])
    gate = _sigmoid(_dot(h.astype(BF16), wg_ref[...]))
    y = gate * e
    o_ref[...] = h + y * _rms_scale(y) * g_ref[...]


def _per_layer_embedding(h, p, layer, w_up, w_gate, g, *, tm_target=512):
    t, d = h.shape
    pd = p.shape[2]
    tm = _tile(t, tm_target, SUBLANES)
    return pl.pallas_call(
        _ple_kernel,
        grid=(t // tm,),
        in_specs=[
            pl.BlockSpec((tm, d), lambda i: (i, 0)),
            pl.BlockSpec((None, tm, pd), lambda i: (layer, i, 0)),
            pl.BlockSpec((pd, d), lambda i: (0, 0)),
            pl.BlockSpec((d, d), lambda i: (0, 0)),
            pl.BlockSpec((1, d), lambda i: (0, 0)),
        ],
        out_specs=pl.BlockSpec((tm, d), lambda i: (i, 0)),
        out_shape=jax.ShapeDtypeStruct((t, d), F32),
        compiler_params=_params("arbitrary"),
        name="per_layer_embedding",
    )(h, p, w_up, w_gate, g.reshape(1, d))


def _sb_attention_kernel(q_ref, k_ref, v_ref, u_ref, o_ref, *, tile, heads, scale):
    i = pl.program_id(2)
    u = u_ref[...]
    qs = [(q_ref[0, :, h * HEAD_DIM:(h + 1) * HEAD_DIM].astype(F32) * scale).astype(BF16)
          for h in range(heads)]
    t_idx = lax.broadcasted_iota(jnp.int32, (tile, tile), 0)
    s_idx = lax.broadcasted_iota(jnp.int32, (tile, tile), 1)
    below_diag = s_idx < t_idx

    def key_tile(j, state, masked):
        start = pl.multiple_of(j * tile, tile)
        hs = range(heads)
        cols = [slice(h * HEAD_DIM, (h + 1) * HEAD_DIM) for h in hs]
        z = [_dot_nt(qs[h], k_ref[0, pl.ds(start, tile), cols[h]]) for h in hs]
        sp = [jnp.maximum(z[h], 0.0) + jnp.log(1.0 + jnp.exp2(-jnp.abs(z[h]))) * LOG2_E for h in hs]
        if masked:
            sp = [jnp.where(below_diag, sp[h], 0.0) for h in hs]
        split = [jnp.concatenate(_split_bf16(sp[h]), axis=1) for h in hs]
        log_rem = [_dot(split[h], u) for h in hs]
        w = [jnp.exp2((z[h] - sp[h]) + log_rem[h] + state[h][1]) for h in hs]
        if masked:
            w = [jnp.where(below_diag, w[h], 0.0) for h in hs]
        pv = [_dot(w[h].astype(BF16), v_ref[0, pl.ds(start, tile), cols[h]]) for h in hs]
        return tuple((state[h][0] + pv[h], state[h][1] - jnp.sum(sp[h], axis=1, keepdims=True)) for h in hs)

    def any_weight_left(state):
        top = functools.reduce(jnp.maximum, [state[h][1] for h in range(heads)])
        return jnp.max(top) > SB_DEAD_LOG2

    state = tuple((jnp.zeros((tile, HEAD_DIM), F32), jnp.zeros((tile, 1), F32)) for _ in range(heads))
    state = key_tile(i, state, True)

    def cond(carry):
        return jnp.logical_and(carry[0] < i, carry[1])

    def body(carry):
        new_state = key_tile(i - 1 - carry[0], carry[2], False)
        return carry[0] + 1, any_weight_left(new_state), new_state

    state = lax.while_loop(cond, body, (jnp.int32(0), any_weight_left(state), state))[2]
    for h in range(heads):
        o_ref[0, :, h * HEAD_DIM:(h + 1) * HEAD_DIM] = state[h][0].astype(o_ref.dtype)


def _sb_attention(proj, n_heads, *, tile_target=256, heads_per_step=4):
    b, s, _ = proj.shape
    tile = _tile(s, tile_target, 128)
    hp = heads_per_step if n_heads % heads_per_step == 0 else 1
    groups = n_heads // hp
    rows = np.arange(tile)
    neg_lower = -(rows[:, None] > rows[None, :]).astype(np.float32)
    u = jnp.asarray(np.concatenate([neg_lower, neg_lower], axis=0), BF16)
    kern = functools.partial(_sb_attention_kernel, tile=tile, heads=hp, scale=HEAD_DIM ** -0.5 * LOG2_E)
    width = hp * HEAD_DIM
    return pl.pallas_call(
        kern,
        grid=(b, groups, s // tile),
        in_specs=[
            pl.BlockSpec((1, tile, width), lambda bi, gi, qi: (bi, qi, gi)),
            pl.BlockSpec((1, s, width), lambda bi, gi, qi: (bi, 0, groups + gi)),
            pl.BlockSpec((1, s, width), lambda bi, gi, qi: (bi, 0, 2 * groups + gi)),
            pl.BlockSpec((2 * tile, tile), lambda bi, gi, qi: (0, 0)),
        ],
        out_specs=pl.BlockSpec((1, tile, width), lambda bi, gi, qi: (bi, qi, gi)),
        out_shape=jax.ShapeDtypeStruct((b, s, n_heads * HEAD_DIM), BF16),
        compiler_params=_params("arbitrary", "arbitrary", "arbitrary"),
        name="sb_attention",
    )(proj, proj, proj, u)


def _hgrn2_constants():
    n = HG_BLOCK
    r = np.arange(n)
    tri = (r[None, :] <= r[:, None])
    mats = [r[None, :] > r[:, None]]
    level = np.where(r[:, None] == r[None, :], 0, -1).astype(np.int32)
    for lv in range(1, HG_LEVELS + 1):
        half = n >> lv
        pos = r % (2 * half)
        bnd = r - pos + half - 1
        upper = pos >= half
        m_q = (r[None, :] > bnd[:, None]) & (r[None, :] <= r[:, None]) & upper[:, None]
        m_k = (r[None, :] > r[:, None]) & (r[None, :] <= bnd[:, None]) & ~upper[:, None]
        mats.append(m_q | m_k)
        same_group = (r[:, None] // (2 * half)) == (r[None, :] // (2 * half))
        pair = same_group & upper[:, None] & ~upper[None, :]
        level = np.where(pair, lv, level)
    m_lev = np.concatenate(mats, axis=0).astype(np.float32)
    return jnp.asarray(tri.astype(np.float32), BF16), jnp.asarray(m_lev, BF16), jnp.asarray(level)


def _hgrn2_kernel(hq_ref, hf_ref, hi_ref, hg_ref, lbl_ref, ng_ref, tri_ref, mlev_ref, lv_ref, o_ref, st_ref,
                  *, n_blocks, layer_j, heads):
    n = HG_BLOCK
    mid_row = n // 2 - 1
    logits = lbl_ref[...]
    e = jnp.exp(logits - jnp.max(logits, axis=0, keepdims=True))
    sm = e / jnp.sum(e, axis=0, keepdims=True)
    lb_all = jnp.sum(sm[: layer_j + 1], axis=0, keepdims=True)
    ng = ng_ref[...]
    st_ref[...] = jnp.zeros_like(st_ref)

    def block(bi, carry):
        r0 = pl.multiple_of(bi * n, n)
        rows = pl.ds(r0, n)
        level = lv_ref[...]

        hs = range(heads)
        col_of = [slice(h * HEAD_DIM, (h + 1) * HEAD_DIM) for h in hs]
        f = [lb_all[:, col_of[h]] + (1.0 - lb_all[:, col_of[h]]) * _sigmoid(hf_ref[0, rows, col_of[h]].astype(F32))
             for h in hs]
        g2 = [jnp.concatenate(_split_bf16(jnp.log(f[h])), axis=1) for h in hs]
        c2 = [_dot(tri_ref[...], g2[h]) for h in hs]
        cum = [c2[h][:, :HEAD_DIM] + c2[h][:, HEAD_DIM:] for h in hs]
        mid = [cum[h][mid_row:mid_row + 1, :] for h in hs]
        last = [cum[h][n - 1:n, :] for h in hs]
        worst = functools.reduce(jnp.minimum, [jnp.minimum(mid[h], last[h] - mid[h]) for h in hs])
        safe = jnp.min(worst) >= -HG_SAFE_EXPONENT
        kk = [1.0 - f[h] for h in hs]
        hq = [hq_ref[0, rows, col_of[h]].astype(F32) for h in hs]
        q = [hq[h] * _sigmoid(hq[h]) for h in hs]
        pre = [(col_of[h], q[h], kk[h], g2[h], cum[h], mid[h], last[h]) for h in hs]

        def finish(h, cols, o):
            hg = hg_ref[0, rows, cols].astype(F32)
            out = o * _rms_scale(o) * ng * (hg * _sigmoid(hg))
            o_ref[0, rows, cols] = out.astype(o_ref.dtype)

        @pl.when(safe)
        def _():
            v = [hi_ref[0, rows, col_of[h]] for h in hs]
            qd = [(q[h] * jnp.exp(cum[h] - mid[h])).astype(BF16) for h in hs]
            kd = [kk[h] * jnp.exp(mid[h] - cum[h]) for h in hs]
            st = [st_ref[h] for h in hs]
            o = [_dot_nt(qd[h], (st[h] * jnp.exp(mid[h])).astype(BF16)) for h in hs]
            sc = [_dot_nt(qd[h], kd[h].astype(BF16)) for h in hs]
            p = [jnp.where(level >= 0, sc[h], 0.0).astype(BF16) for h in hs]
            o = [o[h] + _dot(p[h], v[h]) for h in hs]
            k_dec = [(kd[h] * jnp.exp(last[h] - mid[h])).astype(BF16) for h in hs]
            vt = [v[h].astype(F32).T.astype(BF16) for h in hs]
            for h in hs:
                st_ref[h] = st[h] * jnp.exp(last[h]) + _dot(vt[h], k_dec[h])
                finish(h, col_of[h], o[h])

        @pl.when(jnp.logical_not(safe))
        def _():
            for h, (cols, q, kk, g2, cum, mid, last) in enumerate(pre):
                v = hi_ref[0, rows, cols]
                e2 = _dot(mlev_ref[...], g2)
                x = jnp.exp(e2[:, :HEAD_DIM] + e2[:, HEAD_DIM:])
                st = st_ref[h]
                o = _dot_nt((q * jnp.exp(cum)).astype(BF16), st.astype(BF16))
                p = jnp.where(level == 0, _dot_nt(q.astype(BF16), kk.astype(BF16)), 0.0)
                for lv in range(1, HG_LEVELS + 1):
                    xl = x[lv * n:(lv + 1) * n]
                    sc = _dot_nt((q * xl).astype(BF16), (kk * xl).astype(BF16))
                    p = jnp.where(level == lv, sc, p)
                o = o + _dot(p.astype(BF16), v)
                k_dec = (kk * x[0:n]).astype(BF16)
                vt = v.astype(F32).T.astype(BF16)
                st_ref[h] = st * jnp.exp(last) + _dot(vt, k_dec)
                finish(h, cols, o)

        return carry

    lax.fori_loop(0, n_blocks, block, 0)


def _hgrn2(proj, lb_logits, norm_g, n_heads, col0, layer_j, *, heads_per_step=4):
    b, s, _ = proj.shape
    assert s % HG_BLOCK == 0
    hp = heads_per_step if (n_heads % heads_per_step == 0 and col0 % heads_per_step == 0) else 1
    groups = n_heads // hp
    width = hp * HEAD_DIM
    tri, m_lev, level = _hgrn2_constants()
    n_rows = lb_logits.shape[0]

    def col(off):
        return pl.BlockSpec((1, s, width), lambda bi, gi: (bi, 0, col0 // hp + off * groups + gi))

    kern = functools.partial(_hgrn2_kernel, n_blocks=s // HG_BLOCK, layer_j=layer_j, heads=hp)
    return pl.pallas_call(
        kern,
        grid=(b, groups),
        in_specs=[
            col(0), col(1), col(2), col(3),
            pl.BlockSpec((n_rows, width), lambda bi, gi: (0, gi)),
            pl.BlockSpec((1, HEAD_DIM), lambda bi, gi: (0, 0)),
            pl.BlockSpec(tri.shape, lambda bi, gi: (0, 0)),
            pl.BlockSpec(m_lev.shape, lambda bi, gi: (0, 0)),
            pl.BlockSpec(level.shape, lambda bi, gi: (0, 0)),
        ],
        out_specs=pl.BlockSpec((1, s, width), lambda bi, gi: (bi, 0, gi)),
        out_shape=jax.ShapeDtypeStruct((b, s, n_heads * HEAD_DIM), BF16),
        scratch_shapes=[pltpu.VMEM((hp, HEAD_DIM, HEAD_DIM), F32)],
        compiler_params=_params("arbitrary", "arbitrary"),
        name="hgrn2",
    )(proj, proj, proj, proj, lb_logits, norm_g.reshape(1, HEAD_DIM), tri, m_lev, level)


def _rglru_kernel(gate_ref, x_ref, cw_ref, cb_ref, wax_ref, bax_ref, lam_ref, o_ref, *, chunk, n_chunks):
    w = RG_BLOCK_WIDTH
    cw = cw_ref[...]
    cb = cb_ref[...]
    lam = lam_ref[...]
    half_nc = -0.5 * RG_C * (jnp.maximum(-lam, 0.0) + jnp.log1p(jnp.exp(-jnp.abs(lam))))
    row = lax.broadcasted_iota(jnp.int32, (chunk, w), 0)
    groups = chunk // SUBLANES
    sub = lax.broadcasted_iota(jnp.int32, (groups, SUBLANES, w), 1)
    gelu_c1 = float(np.sqrt(2.0 / np.pi))
    gelu_c3 = gelu_c1 * 0.044715

    def body(ci, carry):
        x_prev, h_prev = carry
        r0 = pl.multiple_of(ci * chunk, chunk)
        rows = pl.ds(r0, chunk)
        xc = x_ref[0, rows, :].astype(F32)
        xh = jnp.concatenate([x_prev, xc], axis=0)
        y = cb + xc * cw[0:1, :]
        for tap in range(1, CONV_WIDTH):
            y = y + pltpu.roll(xh, tap, axis=0)[SUBLANES:, :] * cw[tap:tap + 1, :]
        th = jnp.tanh(_dot(y.astype(BF16), wax_ref[0]) + bax_ref[0])
        log_a = half_nc + half_nc * th[:, :w]
        gi = 0.5 + 0.5 * th[:, w:]
        a = jnp.exp(log_a)
        m2 = -jnp.tanh(log_a) * (a * a + 1.0)
        mult = jnp.where(m2 > 0.0, m2 * lax.rsqrt(m2), 0.0)
        mult = jnp.where(row + r0 == 0, 1.0, mult)
        u = y * gi * mult
        a = a.reshape(groups, SUBLANES, w)
        u = u.reshape(groups, SUBLANES, w)
        for shift in (1, 2, 4):
            keep = sub >= shift
            a_sh = pltpu.roll(a, shift, axis=1)
            u_sh = pltpu.roll(u, shift, axis=1)
            u = jnp.where(keep, u + a * u_sh, u)
            a = jnp.where(keep, a * a_sh, a)
        a = a.reshape(chunk, w)
        u = u.reshape(chunk, w)
        hs = []
        h_last = h_prev
        for gidx in range(chunk // SUBLANES):
            sl = slice(gidx * SUBLANES, (gidx + 1) * SUBLANES)
            hg = u[sl] + a[sl] * h_last
            hs.append(hg)
            h_last = hg[SUBLANES - 1:SUBLANES, :]
        h = jnp.concatenate(hs, axis=0)
        gt = gate_ref[0, rows, :].astype(F32)
        half_gt = 0.5 * gt
        gelu = half_gt + half_gt * jnp.tanh(gt * (gelu_c1 + gelu_c3 * (gt * gt)))
        o_ref[0, rows, :] = (gelu * h).astype(o_ref.dtype)
        return xc[chunk - SUBLANES:, :], h_last

    init = (jnp.zeros((SUBLANES, w), F32), jnp.zeros((1, w), F32))
    lax.fori_loop(0, n_chunks, body, init)


def _rglru(proj, conv_w, conv_b, wa, ba, wx, bx, lam, *, chunk_target=256):
    b, s, w2 = proj.shape
    lru = w2 // 2
    w = RG_BLOCK_WIDTH
    nb = lru // w
    chunk = _tile(s, chunk_target, 2 * SUBLANES)
    wax = (0.5 * jnp.concatenate([wa, wx], axis=-1)).astype(BF16)
    bax = 0.5 * jnp.concatenate([ba, bx], axis=-1).reshape(nb, 1, 2 * w)
    kern = functools.partial(_rglru_kernel, chunk=chunk, n_chunks=s // chunk)
    return pl.pallas_call(
        kern,
        grid=(b, nb),
        in_specs=[
            pl.BlockSpec((1, s, w), lambda bi, ni: (bi, 0, ni)),
            pl.BlockSpec((1, s, w), lambda bi, ni: (bi, 0, nb + ni)),
            pl.BlockSpec((CONV_WIDTH, w), lambda bi, ni: (0, ni)),
            pl.BlockSpec((1, w), lambda bi, ni: (0, ni)),
            pl.BlockSpec((1, w, 2 * w), lambda bi, ni: (ni, 0, 0)),
            pl.BlockSpec((1, 1, 2 * w), lambda bi, ni: (ni, 0, 0)),
            pl.BlockSpec((1, w), lambda bi, ni: (0, ni)),
        ],
        out_specs=pl.BlockSpec((1, s, w), lambda bi, ni: (bi, 0, ni)),
        out_shape=jax.ShapeDtypeStruct((b, s, lru), BF16),
        compiler_params=_params("arbitrary", "arbitrary"),
        name="rglru",
    )(proj, proj, conv_w, conv_b.reshape(1, lru), wax, bax, lam.reshape(1, lru))


def kernel(x, p, mix_pre_g, mix_post_g, ffn_pre_g, ffn_post_g, ple_norm_g, w_in_even, w_out_even, hg_lb_logits, hg_norm_g, w_in_odd, conv_w, conv_b, rg_wa, rg_ba, rg_wx, rg_bx, rg_lambda, w_out_odd, w_gate_up, w_down, w_ple_up, w_ple_gate):
    b, s, d = x.shape
    t = b * s
    depth = p.shape[0]
    h = x.reshape(t, d)
    for i in range(depth):
        j = i // 2
        if i % 2 == 0:
            n_heads = w_out_even.shape[1] // (2 * HEAD_DIM)
            proj = _norm_matmul(h, mix_pre_g[i], w_in_even, j, BF16, tn_target=1024)
            proj = proj.reshape(b, s, -1)
            a_out = _sb_attention(proj, n_heads)
            b_out = _hgrn2(proj, hg_lb_logits, hg_norm_g[j], n_heads, 3 * n_heads, j)
            mixed = jnp.concatenate([a_out, b_out], axis=-1).reshape(t, -1)
            w_out = w_out_even[j].astype(BF16)
        else:
            proj = _norm_matmul(h, mix_pre_g[i], w_in_odd, j, BF16)
            mixed = _rglru(proj.reshape(b, s, -1), conv_w[j], conv_b[j], rg_wa[j], rg_ba[j],
                           rg_wx[j], rg_bx[j], rg_lambda[j]).reshape(t, -1)
            w_out = w_out_odd[j].astype(BF16)
        h = _matmul_norm_residual(mixed, w_out, mix_post_g[i], h, tk_target=2048)
        act = _norm_swiglu(h, ffn_pre_g[i], w_gate_up, i)
        h = _matmul_norm_residual(act, w_down[i].astype(BF16), ffn_post_g[i], h)
        h = _per_layer_embedding(h, p.reshape(depth, t, -1), i, w_ple_up[i].astype(BF16),
                                 w_ple_gate[i].astype(BF16), ple_norm_g[i])
    return h.reshape(b, s, d)
```

```python
import functools

import numpy as np
import jax
import jax.numpy as jnp
from jax import lax
from jax.experimental import pallas as pl
from jax.experimental.pallas import tpu as pltpu

F32 = jnp.float32
BF16 = jnp.bfloat16

RMS_EPS = 1e-6
LOG2_E = float(np.log2(np.e))
HEAD_DIM = 128
RG_BLOCK_WIDTH = 256
CONV_WIDTH = 4
RG_C = 8.0
V7X_VMEM_BYTES = 64 * 1024 * 1024
VMEM_LIMIT_BYTES = V7X_VMEM_BYTES - 8 * 1024 * 1024
SUBLANES = 8
HG_BLOCK = 128
HG_LEVELS = 7
HG_SAFE_EXPONENT = 80.0
SB_DEAD_LOG2 = -160.0


def _tile(n, target, quantum):
    if n <= target:
        return n
    t = (target // quantum) * quantum
    while t > quantum and n % t:
        t -= quantum
    assert n % t == 0, (n, target, quantum)
    return t


def _params(*semantics):
    return pltpu.CompilerParams(dimension_semantics=semantics, vmem_limit_bytes=VMEM_LIMIT_BYTES)


def _sigmoid(x):
    return 0.5 + 0.5 * jnp.tanh(0.5 * x)


def _rms_scale(x):
    return lax.rsqrt(jnp.mean(x * x, axis=-1, keepdims=True) + RMS_EPS)


def _dot(a, b):
    return jnp.dot(a, b, preferred_element_type=F32)


def _dot_nt(a, b):
    return lax.dot_general(a, b, (((1,), (1,)), ((), ())), preferred_element_type=F32)


def _split_bf16(x):
    hi = x.astype(BF16)
    lo = (x - hi.astype(F32)).astype(BF16)
    return hi, lo


def _norm_matmul_kernel(x_ref, g_ref, w_ref, o_ref, xn_ref):
    @pl.when(pl.program_id(1) == 0)
    def _():
        x = x_ref[...]
        xn_ref[...] = (x * _rms_scale(x) * g_ref[...]).astype(BF16)

    o_ref[...] = _dot(xn_ref[...], w_ref[...].astype(BF16)).astype(o_ref.dtype)


def _norm_swiglu_kernel(x_ref, g_ref, wg_ref, wu_ref, o_ref, xn_ref):
    @pl.when(pl.program_id(1) == 0)
    def _():
        x = x_ref[...]
        xn_ref[...] = (x * _rms_scale(x) * g_ref[...]).astype(BF16)

    xn = xn_ref[...]
    gate = _dot(xn, wg_ref[...].astype(BF16))
    up = _dot(xn, wu_ref[...].astype(BF16))
    o_ref[...] = (gate * _sigmoid(gate) * up).astype(o_ref.dtype)


def _norm_matmul(x, g, w, layer, out_dtype, *, tm_target=1024, tn_target=512):
    t, d = x.shape
    n = w.shape[2]
    tm = _tile(t, tm_target, SUBLANES)
    tn = _tile(n, tn_target, 128)
    return pl.pallas_call(
        _norm_matmul_kernel,
        grid=(t // tm, n // tn),
        in_specs=[
            pl.BlockSpec((tm, d), lambda i, j: (i, 0)),
            pl.BlockSpec((1, d), lambda i, j: (0, 0)),
            pl.BlockSpec((None, d, tn), lambda i, j: (layer, 0, j)),
        ],
        out_specs=pl.BlockSpec((tm, tn), lambda i, j: (i, j)),
        out_shape=jax.ShapeDtypeStruct((t, n), out_dtype),
        scratch_shapes=[pltpu.VMEM((tm, d), BF16)],
        compiler_params=_params("arbitrary", "arbitrary"),
        name="norm_matmul",
    )(x, g.reshape(1, d), w)


def _norm_swiglu(x, g, w_gate_up, layer, *, tm_target=1024, tn_target=512):
    t, d = x.shape
    d_ff = w_gate_up.shape[2] // 2
    tm = _tile(t, tm_target, SUBLANES)
    tn = _tile(d_ff, tn_target, 128)
    nj = d_ff // tn
    return pl.pallas_call(
        _norm_swiglu_kernel,
        grid=(t // tm, nj),
        in_specs=[
            pl.BlockSpec((tm, d), lambda i, j: (i, 0)),
            pl.BlockSpec((1, d), lambda i, j: (0, 0)),
            pl.BlockSpec((None, d, tn), lambda i, j: (layer, 0, j)),
            pl.BlockSpec((None, d, tn), lambda i, j: (layer, 0, j + nj)),
        ],
        out_specs=pl.BlockSpec((tm, tn), lambda i, j: (i, j)),
        out_shape=jax.ShapeDtypeStruct((t, d_ff), BF16),
        scratch_shapes=[pltpu.VMEM((tm, d), BF16)],
        compiler_params=_params("arbitrary", "arbitrary"),
        name="norm_swiglu",
    )(x, g.reshape(1, d), w_gate_up, w_gate_up)


def _matmul_norm_residual_kernel(x_ref, w_ref, g_ref, h_ref, o_ref, acc_ref, *, nk):
    k = pl.program_id(1)

    def finish(m):
        o_ref[...] = h_ref[...] + m * _rms_scale(m) * g_ref[...]

    if nk == 1:
        finish(_dot(x_ref[...], w_ref[...]))
        return

    @pl.when(k == 0)
    def _():
        acc_ref[...] = jnp.zeros_like(acc_ref)

    acc_ref[...] += _dot(x_ref[...], w_ref[...])

    @pl.when(k == nk - 1)
    def _():
        finish(acc_ref[...])


def _matmul_norm_residual(x, w, g, h, *, tm_target=512, tk_target=1408):
    t, kdim = x.shape
    d = w.shape[1]
    tm = _tile(t, tm_target, SUBLANES)
    tk = _tile(kdim, tk_target, 128)
    nk = kdim // tk
    return pl.pallas_call(
        functools.partial(_matmul_norm_residual_kernel, nk=nk),
        grid=(t // tm, nk),
        in_specs=[
            pl.BlockSpec((tm, tk), lambda i, k: (i, k)),
            pl.BlockSpec((tk, d), lambda i, k: (k, 0)),
            pl.BlockSpec((1, d), lambda i, k: (0, 0)),
            pl.BlockSpec((tm, d), lambda i, k: (i, 0)),
        ],
        out_specs=pl.BlockSpec((tm, d), lambda i, k: (i, 0)),
        out_shape=jax.ShapeDtypeStruct((t, d), F32),
        scratch_shapes=[pltpu.VMEM((tm, d), F32)],
        compiler_params=_params("arbitrary", "arbitrary"),
        name="matmul_norm_residual",
    )(x, w, g.reshape(1, d), h)


def _ple_kernel(h_ref, p_ref, wup_ref, wg_ref, g_ref, o_ref):
    h = h_ref[...]
    e = _dot(p_ref[...].astype(BF16), wup_ref[...])
    gate = _sigmoid(_dot(h.astype(BF16), wg_ref[...]))
    y = gate * e
    o_ref[...] = h + y * _rms_scale(y) * g_ref[...]


def _per_layer_embedding(h, p, layer, w_up, w_gate, g, *, tm_target=512):
    t, d = h.shape
    pd = p.shape[2]
    tm = _tile(t, tm_target, SUBLANES)
    return pl.pallas_call(
        _ple_kernel,
        grid=(t // tm,),
        in_specs=[
            pl.BlockSpec((tm, d), lambda i: (i, 0)),
            pl.BlockSpec((None, tm, pd), lambda i: (layer, i, 0)),
            pl.BlockSpec((pd, d), lambda i: (0, 0)),
            pl.BlockSpec((d, d), lambda i: (0, 0)),
            pl.BlockSpec((1, d), lambda i: (0, 0)),
        ],
        out_specs=pl.BlockSpec((tm, d), lambda i: (i, 0)),
        out_shape=jax.ShapeDtypeStruct((t, d), F32),
        compiler_params=_params("arbitrary"),
        name="per_layer_embedding",
    )(h, p, w_up, w_gate, g.reshape(1, d))


def _sb_attention_kernel(q_ref, k_ref, v_ref, u_ref, o_ref, *, tile, heads, scale):
    i = pl.program_id(2)
    u = u_ref[...]
    qs = [(q_ref[0, :, h * HEAD_DIM:(h + 1) * HEAD_DIM].astype(F32) * scale).astype(BF16)
          for h in range(heads)]
    t_idx = lax.broadcasted_iota(jnp.int32, (tile, tile), 0)
    s_idx = lax.broadcasted_iota(jnp.int32, (tile, tile), 1)
    below_diag = s_idx < t_idx

    def key_tile(j, state, masked):
        start = pl.multiple_of(j * tile, tile)
        hs = range(heads)
        cols = [slice(h * HEAD_DIM, (h + 1) * HEAD_DIM) for h in hs]
        z = [_dot_nt(qs[h], k_ref[0, pl.ds(start, tile), cols[h]]) for h in hs]
        sp = [jnp.maximum(z[h], 0.0) + jnp.log(1.0 + jnp.exp2(-jnp.abs(z[h]))) * LOG2_E for h in hs]
        if masked:
            sp = [jnp.where(below_diag, sp[h], 0.0) for h in hs]
        split = [jnp.concatenate(_split_bf16(sp[h]), axis=1) for h in hs]
        log_rem = [_dot(split[h], u) for h in hs]
        w = [jnp.exp2((z[h] - sp[h]) + log_rem[h] + state[h][1]) for h in hs]
        if masked:
            w = [jnp.where(below_diag, w[h], 0.0) for h in hs]
        pv = [_dot(w[h].astype(BF16), v_ref[0, pl.ds(start, tile), cols[h]]) for h in hs]
        return tuple((state[h][0] + pv[h], state[h][1] - jnp.sum(sp[h], axis=1, keepdims=True)) for h in hs)

    def any_weight_left(state):
        top = functools.reduce(jnp.maximum, [state[h][1] for h in range(heads)])
        return jnp.max(top) > SB_DEAD_LOG2

    state = tuple((jnp.zeros((tile, HEAD_DIM), F32), jnp.zeros((tile, 1), F32)) for _ in range(heads))
    state = key_tile(i, state, True)

    def cond(carry):
        return jnp.logical_and(carry[0] < i, carry[1])

    def body(carry):
        new_state = key_tile(i - 1 - carry[0], carry[2], False)
        return carry[0] + 1, any_weight_left(new_state), new_state

    state = lax.while_loop(cond, body, (jnp.int32(0), any_weight_left(state), state))[2]
    for h in range(heads):
        o_ref[0, :, h * HEAD_DIM:(h + 1) * HEAD_DIM] = state[h][0].astype(o_ref.dtype)


def _sb_attention(proj, n_heads, *, tile_target=256, heads_per_step=4):
    b, s, _ = proj.shape
    tile = _tile(s, tile_target, 128)
    hp = heads_per_step if n_heads % heads_per_step == 0 else 1
    groups = n_heads // hp
    rows = np.arange(tile)
    neg_lower = -(rows[:, None] > rows[None, :]).astype(np.float32)
    u = jnp.asarray(np.concatenate([neg_lower, neg_lower], axis=0), BF16)
    kern = functools.partial(_sb_attention_kernel, tile=tile, heads=hp, scale=HEAD_DIM ** -0.5 * LOG2_E)
    width = hp * HEAD_DIM
    return pl.pallas_call(
        kern,
        grid=(b, groups, s // tile),
        in_specs=[
            pl.BlockSpec((1, tile, width), lambda bi, gi, qi: (bi, qi, gi)),
            pl.BlockSpec((1, s, width), lambda bi, gi, qi: (bi, 0, groups + gi)),
            pl.BlockSpec((1, s, width), lambda bi, gi, qi: (bi, 0, 2 * groups + gi)),
            pl.BlockSpec((2 * tile, tile), lambda bi, gi, qi: (0, 0)),
        ],
        out_specs=pl.BlockSpec((1, tile, width), lambda bi, gi, qi: (bi, qi, gi)),
        out_shape=jax.ShapeDtypeStruct((b, s, n_heads * HEAD_DIM), BF16),
        compiler_params=_params("arbitrary", "arbitrary", "arbitrary"),
        name="sb_attention",
    )(proj, proj, proj, u)


def _hgrn2_constants():
    n = HG_BLOCK
    r = np.arange(n)
    tri = (r[None, :] <= r[:, None])
    mats = [r[None, :] > r[:, None]]
    level = np.where(r[:, None] == r[None, :], 0, -1).astype(np.int32)
    for lv in range(1, HG_LEVELS + 1):
        half = n >> lv
        pos = r % (2 * half)
        bnd = r - pos + half - 1
        upper = pos >= half
        m_q = (r[None, :] > bnd[:, None]) & (r[None, :] <= r[:, None]) & upper[:, None]
        m_k = (r[None, :] > r[:, None]) & (r[None, :] <= bnd[:, None]) & ~upper[:, None]
        mats.append(m_q | m_k)
        same_group = (r[:, None] // (2 * half)) == (r[None, :] // (2 * half))
        pair = same_group & upper[:, None] & ~upper[None, :]
        level = np.where(pair, lv, level)
    m_lev = np.concatenate(mats, axis=0).astype(np.float32)
    return jnp.asarray(tri.astype(np.float32), BF16), jnp.asarray(m_lev, BF16), jnp.asarray(level)


def _hgrn2_kernel(hq_ref, hf_ref, hi_ref, hg_ref, lbl_ref, ng_ref, tri_ref, mlev_ref, lv_ref, o_ref, st_ref,
                  *, n_blocks, layer_j, heads):
    n = HG_BLOCK
    mid_row = n // 2 - 1
    logits = lbl_ref[...]
    e = jnp.exp(logits - jnp.max(logits, axis=0, keepdims=True))
    sm = e / jnp.sum(e, axis=0, keepdims=True)
    lb_all = jnp.sum(sm[: layer_j + 1], axis=0, keepdims=True)
    ng = ng_ref[...]
    st_ref[...] = jnp.zeros_like(st_ref)

    def block(bi, carry):
        r0 = pl.multiple_of(bi * n, n)
        rows = pl.ds(r0, n)
        level = lv_ref[...]

        hs = range(heads)
        col_of = [slice(h * HEAD_DIM, (h + 1) * HEAD_DIM) for h in hs]
        f = [lb_all[:, col_of[h]] + (1.0 - lb_all[:, col_of[h]]) * _sigmoid(hf_ref[0, rows, col_of[h]].astype(F32))
             for h in hs]
        g2 = [jnp.concatenate(_split_bf16(jnp.log(f[h])), axis=1) for h in hs]
        c2 = [_dot(tri_ref[...], g2[h]) for h in hs]
        cum = [c2[h][:, :HEAD_DIM] + c2[h][:, HEAD_DIM:] for h in hs]
        mid = [cum[h][mid_row:mid_row + 1, :] for h in hs]
        last = [cum[h][n - 1:n, :] for h in hs]
        worst = functools.reduce(jnp.minimum, [jnp.minimum(mid[h], last[h] - mid[h]) for h in hs])
        safe = jnp.min(worst) >= -HG_SAFE_EXPONENT
        kk = [1.0 - f[h] for h in hs]
        hq = [hq_ref[0, rows, col_of[h]].astype(F32) for h in hs]
        q = [hq[h] * _sigmoid(hq[h]) for h in hs]
        pre = [(col_of[h], q[h], kk[h], g2[h], cum[h], mid[h], last[h]) for h in hs]

        def finish(h, cols, o):
            hg = hg_ref[0, rows, cols].astype(F32)
            out = o * _rms_scale(o) * ng * (hg * _sigmoid(hg))
            o_ref[0, rows, cols] = out.astype(o_ref.dtype)

        @pl.when(safe)
        def _():
            v = [hi_ref[0, rows, col_of[h]] for h in hs]
            qd = [(q[h] * jnp.exp(cum[h] - mid[h])).astype(BF16) for h in hs]
            kd = [kk[h] * jnp.exp(mid[h] - cum[h]) for h in hs]
            st = [st_ref[h] for h in hs]
            o = [_dot_nt(qd[h], (st[h] * jnp.exp(mid[h])).astype(BF16)) for h in hs]
            sc = [_dot_nt(qd[h], kd[h].astype(BF16)) for h in hs]
            p = [jnp.where(level >= 0, sc[h], 0.0).astype(BF16) for h in hs]
            o = [o[h] + _dot(p[h], v[h]) for h in hs]
            k_dec = [(kd[h] * jnp.exp(last[h] - mid[h])).astype(BF16) for h in hs]
            vt = [v[h].astype(F32).T.astype(BF16) for h in hs]
            for h in hs:
                st_ref[h] = st[h] * jnp.exp(last[h]) + _dot(vt[h], k_dec[h])
                finish(h, col_of[h], o[h])

        @pl.when(jnp.logical_not(safe))
        def _():
            for h, (cols, q, kk, g2, cum, mid, last) in enumerate(pre):
                v = hi_ref[0, rows, cols]
                e2 = _dot(mlev_ref[...], g2)
                x = jnp.exp(e2[:, :HEAD_DIM] + e2[:, HEAD_DIM:])
                st = st_ref[h]
                o = _dot_nt((q * jnp.exp(cum)).astype(BF16), st.astype(BF16))
                p = jnp.where(level == 0, _dot_nt(q.astype(BF16), kk.astype(BF16)), 0.0)
                for lv in range(1, HG_LEVELS + 1):
                    xl = x[lv * n:(lv + 1) * n]
                    sc = _dot_nt((q * xl).astype(BF16), (kk * xl).astype(BF16))
                    p = jnp.where(level == lv, sc, p)
                o = o + _dot(p.astype(BF16), v)
                k_dec = (kk * x[0:n]).astype(BF16)
                vt = v.astype(F32).T.astype(BF16)
                st_ref[h] = st * jnp.exp(last) + _dot(vt, k_dec)
                finish(h, cols, o)

        return carry

    lax.fori_loop(0, n_blocks, block, 0)


def _hgrn2(proj, lb_logits, norm_g, n_heads, col0, layer_j, *, heads_per_step=8):
    b, s, _ = proj.shape
    assert s % HG_BLOCK == 0
    hp = heads_per_step if (n_heads % heads_per_step == 0 and col0 % heads_per_step == 0) else 1
    groups = n_heads // hp
    width = hp * HEAD_DIM
    tri, m_lev, level = _hgrn2_constants()
    n_rows = lb_logits.shape[0]

    def col(off):
        return pl.BlockSpec((1, s, width), lambda bi, gi: (bi, 0, col0 // hp + off * groups + gi))

    kern = functools.partial(_hgrn2_kernel, n_blocks=s // HG_BLOCK, layer_j=layer_j, heads=hp)
    return pl.pallas_call(
        kern,
        grid=(b, groups),
        in_specs=[
            col(0), col(1), col(2), col(3),
            pl.BlockSpec((n_rows, width), lambda bi, gi: (0, gi)),
            pl.BlockSpec((1, HEAD_DIM), lambda bi, gi: (0, 0)),
            pl.BlockSpec(tri.shape, lambda bi, gi: (0, 0)),
            pl.BlockSpec(m_lev.shape, lambda bi, gi: (0, 0)),
            pl.BlockSpec(level.shape, lambda bi, gi: (0, 0)),
        ],
        out_specs=pl.BlockSpec((1, s, width), lambda bi, gi: (bi, 0, gi)),
        out_shape=jax.ShapeDtypeStruct((b, s, n_heads * HEAD_DIM), BF16),
        scratch_shapes=[pltpu.VMEM((hp, HEAD_DIM, HEAD_DIM), F32)],
        compiler_params=_params("arbitrary", "arbitrary"),
        name="hgrn2",
    )(proj, proj, proj, proj, lb_logits, norm_g.reshape(1, HEAD_DIM), tri, m_lev, level)


def _rglru_kernel(gate_ref, x_ref, cw_ref, cb_ref, wax_ref, bax_ref, lam_ref, o_ref, *, chunk, n_chunks):
    w = RG_BLOCK_WIDTH
    cw = cw_ref[...]
    cb = cb_ref[...]
    lam = lam_ref[...]
    half_nc = -0.5 * RG_C * (jnp.maximum(-lam, 0.0) + jnp.log1p(jnp.exp(-jnp.abs(lam))))
    row = lax.broadcasted_iota(jnp.int32, (chunk, w), 0)
    groups = chunk // SUBLANES
    sub = lax.broadcasted_iota(jnp.int32, (groups, SUBLANES, w), 1)
    gelu_c1 = float(np.sqrt(2.0 / np.pi))
    gelu_c3 = gelu_c1 * 0.044715

    def body(ci, carry):
        x_prev, h_prev = carry
        r0 = pl.multiple_of(ci * chunk, chunk)
        rows = pl.ds(r0, chunk)
        xc = x_ref[0, rows, :].astype(F32)
        xh = jnp.concatenate([x_prev, xc], axis=0)
        y = cb + xc * cw[0:1, :]
        for tap in range(1, CONV_WIDTH):
            y = y + pltpu.roll(xh, tap, axis=0)[SUBLANES:, :] * cw[tap:tap + 1, :]
        th = jnp.tanh(_dot(y.astype(BF16), wax_ref[0]) + bax_ref[0])
        log_a = half_nc + half_nc * th[:, :w]
        gi = 0.5 + 0.5 * th[:, w:]
        a = jnp.exp(log_a)
        m2 = -jnp.tanh(log_a) * (a * a + 1.0)
        mult = jnp.where(m2 > 0.0, m2 * lax.rsqrt(m2), 0.0)
        mult = jnp.where(row + r0 == 0, 1.0, mult)
        u = y * gi * mult
        a = a.reshape(groups, SUBLANES, w)
        u = u.reshape(groups, SUBLANES, w)
        for shift in (1, 2, 4):
            keep = sub >= shift
            a_sh = pltpu.roll(a, shift, axis=1)
            u_sh = pltpu.roll(u, shift, axis=1)
            u = jnp.where(keep, u + a * u_sh, u)
            a = jnp.where(keep, a * a_sh, a)
        a = a.reshape(chunk, w)
        u = u.reshape(chunk, w)
        hs = []
        h_last = h_prev
        for gidx in range(chunk // SUBLANES):
            sl = slice(gidx * SUBLANES, (gidx + 1) * SUBLANES)
            hg = u[sl] + a[sl] * h_last
            hs.append(hg)
            h_last = hg[SUBLANES - 1:SUBLANES, :]
        h = jnp.concatenate(hs, axis=0)
        gt = gate_ref[0, rows, :].astype(F32)
        half_gt = 0.5 * gt
        gelu = half_gt + half_gt * jnp.tanh(gt * (gelu_c1 + gelu_c3 * (gt * gt)))
        o_ref[0, rows, :] = (gelu * h).astype(o_ref.dtype)
        return xc[chunk - SUBLANES:, :], h_last

    init = (jnp.zeros((SUBLANES, w), F32), jnp.zeros((1, w), F32))
    lax.fori_loop(0, n_chunks, body, init)


def _rglru(proj, conv_w, conv_b, wa, ba, wx, bx, lam, *, chunk_target=256):
    b, s, w2 = proj.shape
    lru = w2 // 2
    w = RG_BLOCK_WIDTH
    nb = lru // w
    chunk = _tile(s, chunk_target, 2 * SUBLANES)
    wax = (0.5 * jnp.concatenate([wa, wx], axis=-1)).astype(BF16)
    bax = 0.5 * jnp.concatenate([ba, bx], axis=-1).reshape(nb, 1, 2 * w)
    kern = functools.partial(_rglru_kernel, chunk=chunk, n_chunks=s // chunk)
    return pl.pallas_call(
        kern,
        grid=(b, nb),
        in_specs=[
            pl.BlockSpec((1, s, w), lambda bi, ni: (bi, 0, ni)),
            pl.BlockSpec((1, s, w), lambda bi, ni: (bi, 0, nb + ni)),
            pl.BlockSpec((CONV_WIDTH, w), lambda bi, ni: (0, ni)),
            pl.BlockSpec((1, w), lambda bi, ni: (0, ni)),
            pl.BlockSpec((1, w, 2 * w), lambda bi, ni: (ni, 0, 0)),
            pl.BlockSpec((1, 1, 2 * w), lambda bi, ni: (ni, 0, 0)),
            pl.BlockSpec((1, w), lambda bi, ni: (0, ni)),
        ],
        out_specs=pl.BlockSpec((1, s, w), lambda bi, ni: (bi, 0, ni)),
        out_shape=jax.ShapeDtypeStruct((b, s, lru), BF16),
        compiler_params=_params("arbitrary", "arbitrary"),
        name="rglru",
    )(proj, proj, conv_w, conv_b.reshape(1, lru), wax, bax, lam.reshape(1, lru))


def kernel(x, p, mix_pre_g, mix_post_g, ffn_pre_g, ffn_post_g, ple_norm_g, w_in_even, w_out_even, hg_lb_logits, hg_norm_g, w_in_odd, conv_w, conv_b, rg_wa, rg_ba, rg_wx, rg_bx, rg_lambda, w_out_odd, w_gate_up, w_down, w_ple_up, w_ple_gate):
    b, s, d = x.shape
    t = b * s
    depth = p.shape[0]
    h = x.reshape(t, d)
    for i in range(depth):
        j = i // 2
        if i % 2 == 0:
            n_heads = w_out_even.shape[1] // (2 * HEAD_DIM)
            proj = _norm_matmul(h, mix_pre_g[i], w_in_even, j, BF16, tn_target=1024)
            proj = proj.reshape(b, s, -1)
            a_out = _sb_attention(proj, n_heads)
            b_out = _hgrn2(proj, hg_lb_logits, hg_norm_g[j], n_heads, 3 * n_heads, j)
            mixed = jnp.concatenate([a_out, b_out], axis=-1).reshape(t, -1)
            w_out = w_out_even[j].astype(BF16)
        else:
            proj = _norm_matmul(h, mix_pre_g[i], w_in_odd, j, BF16)
            mixed = _rglru(proj.reshape(b, s, -1), conv_w[j], conv_b[j], rg_wa[j], rg_ba[j],
                           rg_wx[j], rg_bx[j], rg_lambda[j]).reshape(t, -1)
            w_out = w_out_odd[j].astype(BF16)
        h = _matmul_norm_residual(mixed, w_out, mix_post_g[i], h, tk_target=2048)
        act = _norm_swiglu(h, ffn_pre_g[i], w_gate_up, i)
        h = _matmul_norm_residual(act, w_down[i].astype(BF16), ffn_post_g[i], h)
        h = _per_layer_embedding(h, p.reshape(depth, t, -1), i, w_ple_up[i].astype(BF16),
                                 w_ple_gate[i].astype(BF16), ple_norm_g[i])
    return h.reshape(b, s, d)
```

```python
import functools

import numpy as np
import jax
import jax.numpy as jnp
from jax import lax
from jax.experimental import pallas as pl
from jax.experimental.pallas import tpu as pltpu

F32 = jnp.float32
BF16 = jnp.bfloat16

RMS_EPS = 1e-6
LOG2_E = float(np.log2(np.e))
HEAD_DIM = 128
RG_BLOCK_WIDTH = 256
CONV_WIDTH = 4
CONV_HALO = 16
RG_C = 8.0
V7X_VMEM_BYTES = 64 * 1024 * 1024
VMEM_LIMIT_BYTES = V7X_VMEM_BYTES - 8 * 1024 * 1024
SUBLANES = 8
HG_BLOCK = 128
HG_LEVELS = 7
HG_SAFE_EXPONENT = 80.0
SB_DEAD_LOG2 = -160.0


def _tile(n, target, quantum):
    if n <= target:
        return n
    t = (target // quantum) * quantum
    while t > quantum and n % t:
        t -= quantum
    assert n % t == 0, (n, target, quantum)
    return t


def _params(*semantics):
    return pltpu.CompilerParams(dimension_semantics=semantics, vmem_limit_bytes=VMEM_LIMIT_BYTES)


def _sigmoid(x):
    return 0.5 + 0.5 * jnp.tanh(0.5 * x)


def _rms_scale(x):
    return lax.rsqrt(jnp.mean(x * x, axis=-1, keepdims=True) + RMS_EPS)


def _dot(a, b):
    return jnp.dot(a, b, preferred_element_type=F32)


def _dot_nt(a, b):
    return lax.dot_general(a, b, (((1,), (1,)), ((), ())), preferred_element_type=F32)


def _split_bf16(x):
    hi = x.astype(BF16)
    lo = (x - hi.astype(F32)).astype(BF16)
    return hi, lo


def _norm_matmul_kernel(x_ref, g_ref, w_ref, o_ref, xn_ref):
    @pl.when(pl.program_id(1) == 0)
    def _():
        x = x_ref[...]
        xn_ref[...] = (x * _rms_scale(x) * g_ref[...]).astype(BF16)

    o_ref[...] = _dot(xn_ref[...], w_ref[...].astype(BF16)).astype(o_ref.dtype)


def _norm_swiglu_kernel(x_ref, g_ref, wg_ref, wu_ref, o_ref, xn_ref):
    @pl.when(pl.program_id(1) == 0)
    def _():
        x = x_ref[...]
        xn_ref[...] = (x * _rms_scale(x) * g_ref[...]).astype(BF16)

    xn = xn_ref[...]
    gate = _dot(xn, wg_ref[...].astype(BF16))
    up = _dot(xn, wu_ref[...].astype(BF16))
    o_ref[...] = (gate * _sigmoid(gate) * up).astype(o_ref.dtype)


def _norm_matmul(x, g, w, layer, out_dtype, *, tm_target=1024, tn_target=512):
    t, d = x.shape
    n = w.shape[2]
    tm = _tile(t, tm_target, SUBLANES)
    tn = _tile(n, tn_target, 128)
    return pl.pallas_call(
        _norm_matmul_kernel,
        grid=(t // tm, n // tn),
        in_specs=[
            pl.BlockSpec((tm, d), lambda i, j: (i, 0)),
            pl.BlockSpec((1, d), lambda i, j: (0, 0)),
            pl.BlockSpec((None, d, tn), lambda i, j: (layer, 0, j)),
        ],
        out_specs=pl.BlockSpec((tm, tn), lambda i, j: (i, j)),
        out_shape=jax.ShapeDtypeStruct((t, n), out_dtype),
        scratch_shapes=[pltpu.VMEM((tm, d), BF16)],
        compiler_params=_params("arbitrary", "arbitrary"),
        name="norm_matmul",
    )(x, g.reshape(1, d), w)


def _norm_swiglu(x, g, w_gate_up, layer, *, tm_target=1024, tn_target=512):
    t, d = x.shape
    d_ff = w_gate_up.shape[2] // 2
    tm = _tile(t, tm_target, SUBLANES)
    tn = _tile(d_ff, tn_target, 128)
    nj = d_ff // tn
    return pl.pallas_call(
        _norm_swiglu_kernel,
        grid=(t // tm, nj),
        in_specs=[
            pl.BlockSpec((tm, d), lambda i, j: (i, 0)),
            pl.BlockSpec((1, d), lambda i, j: (0, 0)),
            pl.BlockSpec((None, d, tn), lambda i, j: (layer, 0, j)),
            pl.BlockSpec((None, d, tn), lambda i, j: (layer, 0, j + nj)),
        ],
        out_specs=pl.BlockSpec((tm, tn), lambda i, j: (i, j)),
        out_shape=jax.ShapeDtypeStruct((t, d_ff), BF16),
        scratch_shapes=[pltpu.VMEM((tm, d), BF16)],
        compiler_params=_params("arbitrary", "arbitrary"),
        name="norm_swiglu",
    )(x, g.reshape(1, d), w_gate_up, w_gate_up)


def _matmul_norm_residual_kernel(x_ref, w_ref, g_ref, h_ref, o_ref, acc_ref, *, nk):
    k = pl.program_id(1)

    def finish(m):
        o_ref[...] = h_ref[...] + m * _rms_scale(m) * g_ref[...]

    if nk == 1:
        finish(_dot(x_ref[...], w_ref[...]))
        return

    @pl.when(k == 0)
    def _():
        acc_ref[...] = jnp.zeros_like(acc_ref)

    acc_ref[...] += _dot(x_ref[...], w_ref[...])

    @pl.when(k == nk - 1)
    def _():
        finish(acc_ref[...])


def _matmul_norm_residual(x, w, g, h, *, tm_target=512, tk_target=1408):
    t, kdim = x.shape
    d = w.shape[1]
    tm = _tile(t, tm_target, SUBLANES)
    tk = _tile(kdim, tk_target, 128)
    nk = kdim // tk
    return pl.pallas_call(
        functools.partial(_matmul_norm_residual_kernel, nk=nk),
        grid=(t // tm, nk),
        in_specs=[
            pl.BlockSpec((tm, tk), lambda i, k: (i, k)),
            pl.BlockSpec((tk, d), lambda i, k: (k, 0)),
            pl.BlockSpec((1, d), lambda i, k: (0, 0)),
            pl.BlockSpec((tm, d), lambda i, k: (i, 0)),
        ],
        out_specs=pl.BlockSpec((tm, d), lambda i, k: (i, 0)),
        out_shape=jax.ShapeDtypeStruct((t, d), F32),
        scratch_shapes=[pltpu.VMEM((tm, d), F32)],
        compiler_params=_params("arbitrary", "arbitrary"),
        name="matmul_norm_residual",
    )(x, w, g.reshape(1, d), h)


def _ple_kernel(h_ref, p_ref, wup_ref, wg_ref, g_ref, o_ref):
    h = h_ref[...]
    e = _dot(p_ref[...].astype(BF16), wup_ref[...])
    gate = _sigmoid(_dot(h.astype(BF16), wg_ref[...]))
    y = gate * e
    o_ref[...] = h + y * _rms_scale(y) * g_ref[...]


def _per_layer_embedding(h, p, layer, w_up, w_gate, g, *, tm_target=512):
    t, d = h.shape
    pd = p.shape[2]
    tm = _tile(t, tm_target, SUBLANES)
    return pl.pallas_call(
        _ple_kernel,
        grid=(t // tm,),
        in_specs=[
            pl.BlockSpec((tm, d), lambda i: (i, 0)),
            pl.BlockSpec((None, tm, pd), lambda i: (layer, i, 0)),
            pl.BlockSpec((pd, d), lambda i: (0, 0)),
            pl.BlockSpec((d, d), lambda i: (0, 0)),
            pl.BlockSpec((1, d), lambda i: (0, 0)),
        ],
        out_specs=pl.BlockSpec((tm, d), lambda i: (i, 0)),
        out_shape=jax.ShapeDtypeStruct((t, d), F32),
        compiler_params=_params("arbitrary"),
        name="per_layer_embedding",
    )(h, p, w_up, w_gate, g.reshape(1, d))


def _sb_attention_kernel(q_ref, k_ref, v_ref, u_ref, o_ref, *, tile, heads, scale):
    i = pl.program_id(2)
    u = u_ref[...]
    qs = [(q_ref[0, :, h * HEAD_DIM:(h + 1) * HEAD_DIM].astype(F32) * scale).astype(BF16)
          for h in range(heads)]
    t_idx = lax.broadcasted_iota(jnp.int32, (tile, tile), 0)
    s_idx = lax.broadcasted_iota(jnp.int32, (tile, tile), 1)
    below_diag = s_idx < t_idx

    def key_tile(j, state, masked):
        start = pl.multiple_of(j * tile, tile)
        hs = range(heads)
        cols = [slice(h * HEAD_DIM, (h + 1) * HEAD_DIM) for h in hs]
        z = [_dot_nt(qs[h], k_ref[0, pl.ds(start, tile), cols[h]]) for h in hs]
        sp = [jnp.maximum(z[h], 0.0) + jnp.log(1.0 + jnp.exp2(-jnp.abs(z[h]))) * LOG2_E for h in hs]
        if masked:
            sp = [jnp.where(below_diag, sp[h], 0.0) for h in hs]
        split = [jnp.concatenate(_split_bf16(sp[h]), axis=1) for h in hs]
        log_rem = [_dot(split[h], u) for h in hs]
        w = [jnp.exp2((z[h] - sp[h]) + log_rem[h] + state[h][1]) for h in hs]
        if masked:
            w = [jnp.where(below_diag, w[h], 0.0) for h in hs]
        pv = [_dot(w[h].astype(BF16), v_ref[0, pl.ds(start, tile), cols[h]]) for h in hs]
        return tuple((state[h][0] + pv[h], state[h][1] - jnp.sum(sp[h], axis=1, keepdims=True)) for h in hs)

    def any_weight_left(state):
        top = functools.reduce(jnp.maximum, [state[h][1] for h in range(heads)])
        return jnp.max(top) > SB_DEAD_LOG2

    state = tuple((jnp.zeros((tile, HEAD_DIM), F32), jnp.zeros((tile, 1), F32)) for _ in range(heads))
    state = key_tile(i, state, True)

    def cond(carry):
        return jnp.logical_and(carry[0] < i, carry[1])

    def body(carry):
        new_state = key_tile(i - 1 - carry[0], carry[2], False)
        return carry[0] + 1, any_weight_left(new_state), new_state

    state = lax.while_loop(cond, body, (jnp.int32(0), any_weight_left(state), state))[2]
    for h in range(heads):
        o_ref[0, :, h * HEAD_DIM:(h + 1) * HEAD_DIM] = state[h][0].astype(o_ref.dtype)


def _sb_attention(proj, n_heads, *, tile_target=256, heads_per_step=4):
    b, s, _ = proj.shape
    tile = _tile(s, tile_target, 128)
    hp = heads_per_step if n_heads % heads_per_step == 0 else 1
    groups = n_heads // hp
    rows = np.arange(tile)
    neg_lower = -(rows[:, None] > rows[None, :]).astype(np.float32)
    u = jnp.asarray(np.concatenate([neg_lower, neg_lower], axis=0), BF16)
    kern = functools.partial(_sb_attention_kernel, tile=tile, heads=hp, scale=HEAD_DIM ** -0.5 * LOG2_E)
    width = hp * HEAD_DIM
    return pl.pallas_call(
        kern,
        grid=(b, groups, s // tile),
        in_specs=[
            pl.BlockSpec((1, tile, width), lambda bi, gi, qi: (bi, qi, gi)),
            pl.BlockSpec((1, s, width), lambda bi, gi, qi: (bi, 0, groups + gi)),
            pl.BlockSpec((1, s, width), lambda bi, gi, qi: (bi, 0, 2 * groups + gi)),
            pl.BlockSpec((2 * tile, tile), lambda bi, gi, qi: (0, 0)),
        ],
        out_specs=pl.BlockSpec((1, tile, width), lambda bi, gi, qi: (bi, qi, gi)),
        out_shape=jax.ShapeDtypeStruct((b, s, n_heads * HEAD_DIM), BF16),
        compiler_params=_params("arbitrary", "arbitrary", "arbitrary"),
        name="sb_attention",
    )(proj, proj, proj, u)


def _hgrn2_constants():
    n = HG_BLOCK
    r = np.arange(n)
    tri = (r[None, :] <= r[:, None])
    mats = [r[None, :] > r[:, None]]
    level = np.where(r[:, None] == r[None, :], 0, -1).astype(np.int32)
    for lv in range(1, HG_LEVELS + 1):
        half = n >> lv
        pos = r % (2 * half)
        bnd = r - pos + half - 1
        upper = pos >= half
        m_q = (r[None, :] > bnd[:, None]) & (r[None, :] <= r[:, None]) & upper[:, None]
        m_k = (r[None, :] > r[:, None]) & (r[None, :] <= bnd[:, None]) & ~upper[:, None]
        mats.append(m_q | m_k)
        same_group = (r[:, None] // (2 * half)) == (r[None, :] // (2 * half))
        pair = same_group & upper[:, None] & ~upper[None, :]
        level = np.where(pair, lv, level)
    m_lev = np.concatenate(mats, axis=0).astype(np.float32)
    return jnp.asarray(tri.astype(np.float32), BF16), jnp.asarray(m_lev, BF16), jnp.asarray(level)


def _hgrn2_kernel(hq_ref, hf_ref, hi_ref, hg_ref, lbl_ref, ng_ref, tri_ref, mlev_ref, lv_ref, o_ref, st_ref,
                  *, n_blocks, layer_j, heads):
    n = HG_BLOCK
    mid_row = n // 2 - 1
    logits = lbl_ref[...]
    e = jnp.exp(logits - jnp.max(logits, axis=0, keepdims=True))
    sm = e / jnp.sum(e, axis=0, keepdims=True)
    lb_all = jnp.sum(sm[: layer_j + 1], axis=0, keepdims=True)
    ng = ng_ref[...]
    st_ref[...] = jnp.zeros_like(st_ref)

    def block(bi, carry):
        r0 = pl.multiple_of(bi * n, n)
        rows = pl.ds(r0, n)
        level = lv_ref[...]

        hs = range(heads)
        col_of = [slice(h * HEAD_DIM, (h + 1) * HEAD_DIM) for h in hs]
        f = [lb_all[:, col_of[h]] + (1.0 - lb_all[:, col_of[h]]) * _sigmoid(hf_ref[0, rows, col_of[h]].astype(F32))
             for h in hs]
        g2 = [jnp.concatenate(_split_bf16(jnp.log(f[h])), axis=1) for h in hs]
        c2 = [_dot(tri_ref[...], g2[h]) for h in hs]
        cum = [c2[h][:, :HEAD_DIM] + c2[h][:, HEAD_DIM:] for h in hs]
        mid = [cum[h][mid_row:mid_row + 1, :] for h in hs]
        last = [cum[h][n - 1:n, :] for h in hs]
        worst = functools.reduce(jnp.minimum, [jnp.minimum(mid[h], last[h] - mid[h]) for h in hs])
        safe = jnp.min(worst) >= -HG_SAFE_EXPONENT
        kk = [1.0 - f[h] for h in hs]
        hq = [hq_ref[0, rows, col_of[h]].astype(F32) for h in hs]
        q = [hq[h] * _sigmoid(hq[h]) for h in hs]
        pre = [(col_of[h], q[h], kk[h], g2[h], cum[h], mid[h], last[h]) for h in hs]

        def finish(h, cols, o):
            hg = hg_ref[0, rows, cols].astype(F32)
            out = o * _rms_scale(o) * ng * (hg * _sigmoid(hg))
            o_ref[0, rows, cols] = out.astype(o_ref.dtype)

        @pl.when(safe)
        def _():
            v = [hi_ref[0, rows, col_of[h]] for h in hs]
            qd = [(q[h] * jnp.exp(cum[h] - mid[h])).astype(BF16) for h in hs]
            kd = [kk[h] * jnp.exp(mid[h] - cum[h]) for h in hs]
            st = [st_ref[h] for h in hs]
            o = [_dot_nt(qd[h], (st[h] * jnp.exp(mid[h])).astype(BF16)) for h in hs]
            sc = [_dot_nt(qd[h], kd[h].astype(BF16)) for h in hs]
            p = [jnp.where(level >= 0, sc[h], 0.0).astype(BF16) for h in hs]
            o = [o[h] + _dot(p[h], v[h]) for h in hs]
            k_dec = [(kd[h] * jnp.exp(last[h] - mid[h])).astype(BF16) for h in hs]
            vt = [v[h].astype(F32).T.astype(BF16) for h in hs]
            for h in hs:
                st_ref[h] = st[h] * jnp.exp(last[h]) + _dot(vt[h], k_dec[h])
                finish(h, col_of[h], o[h])

        @pl.when(jnp.logical_not(safe))
        def _():
            for h, (cols, q, kk, g2, cum, mid, last) in enumerate(pre):
                v = hi_ref[0, rows, cols]
                e2 = _dot(mlev_ref[...], g2)
                x = jnp.exp(e2[:, :HEAD_DIM] + e2[:, HEAD_DIM:])
                st = st_ref[h]
                o = _dot_nt((q * jnp.exp(cum)).astype(BF16), st.astype(BF16))
                p = jnp.where(level == 0, _dot_nt(q.astype(BF16), kk.astype(BF16)), 0.0)
                for lv in range(1, HG_LEVELS + 1):
                    xl = x[lv * n:(lv + 1) * n]
                    sc = _dot_nt((q * xl).astype(BF16), (kk * xl).astype(BF16))
                    p = jnp.where(level == lv, sc, p)
                o = o + _dot(p.astype(BF16), v)
                k_dec = (kk * x[0:n]).astype(BF16)
                vt = v.astype(F32).T.astype(BF16)
                st_ref[h] = st * jnp.exp(last) + _dot(vt, k_dec)
                finish(h, cols, o)

        return carry

    lax.fori_loop(0, n_blocks, block, 0)


def _hgrn2(proj, lb_logits, norm_g, n_heads, col0, layer_j, *, heads_per_step=8):
    b, s, _ = proj.shape
    assert s % HG_BLOCK == 0
    hp = heads_per_step if (n_heads % heads_per_step == 0 and col0 % heads_per_step == 0) else 1
    groups = n_heads // hp
    width = hp * HEAD_DIM
    tri, m_lev, level = _hgrn2_constants()
    n_rows = lb_logits.shape[0]

    def col(off):
        return pl.BlockSpec((1, s, width), lambda bi, gi: (bi, 0, col0 // hp + off * groups + gi))

    kern = functools.partial(_hgrn2_kernel, n_blocks=s // HG_BLOCK, layer_j=layer_j, heads=hp)
    return pl.pallas_call(
        kern,
        grid=(b, groups),
        in_specs=[
            col(0), col(1), col(2), col(3),
            pl.BlockSpec((n_rows, width), lambda bi, gi: (0, gi)),
            pl.BlockSpec((1, HEAD_DIM), lambda bi, gi: (0, 0)),
            pl.BlockSpec(tri.shape, lambda bi, gi: (0, 0)),
            pl.BlockSpec(m_lev.shape, lambda bi, gi: (0, 0)),
            pl.BlockSpec(level.shape, lambda bi, gi: (0, 0)),
        ],
        out_specs=pl.BlockSpec((1, s, width), lambda bi, gi: (bi, 0, gi)),
        out_shape=jax.ShapeDtypeStruct((b, s, n_heads * HEAD_DIM), BF16),
        scratch_shapes=[pltpu.VMEM((hp, HEAD_DIM, HEAD_DIM), F32)],
        compiler_params=_params("arbitrary", "arbitrary"),
        name="hgrn2",
    )(proj, proj, proj, proj, lb_logits, norm_g.reshape(1, HEAD_DIM), tri, m_lev, level)


def _rglru_kernel(gate_ref, x_ref, cw_ref, cb_ref, wax_ref, bax_ref, lam_ref, shift_ref, o_ref, *, chunk, n_chunks):
    w = RG_BLOCK_WIDTH
    cw = cw_ref[...]
    cb = cb_ref[...]
    lam = lam_ref[...]
    half_nc = -0.5 * RG_C * (jnp.maximum(-lam, 0.0) + jnp.log1p(jnp.exp(-jnp.abs(lam))))
    row = lax.broadcasted_iota(jnp.int32, (chunk, w), 0)
    groups = chunk // SUBLANES
    sub = lax.broadcasted_iota(jnp.int32, (groups, SUBLANES, w), 1)
    gelu_c1 = float(np.sqrt(2.0 / np.pi))
    gelu_c3 = gelu_c1 * 0.044715

    def body(ci, carry):
        x_prev, h_prev = carry
        r0 = pl.multiple_of(ci * chunk, chunk)
        rows = pl.ds(r0, chunk)
        xb = x_ref[0, rows, :]
        xc = xb.astype(F32)
        head = jnp.concatenate([x_prev.astype(F32), xc[:CONV_HALO]], axis=0)
        y = cb + xc * cw[0:1, :]
        y_head = y[:CONV_HALO]
        for tap in range(1, CONV_WIDTH):
            y = y + _dot(shift_ref[tap - 1], xb) * cw[tap:tap + 1, :]
            y_head = y_head + pltpu.roll(head, tap, axis=0)[CONV_HALO:, :] * cw[tap:tap + 1, :]
        y = jnp.concatenate([y_head, y[CONV_HALO:]], axis=0)
        th = jnp.tanh(_dot(y.astype(BF16), wax_ref[0]) + bax_ref[0])
        log_a = half_nc + half_nc * th[:, :w]
        gi = 0.5 + 0.5 * th[:, w:]
        a = jnp.exp(log_a)
        m2 = -jnp.tanh(log_a) * (a * a + 1.0)
        mult = jnp.where(m2 > 0.0, m2 * lax.rsqrt(m2), 0.0)
        mult = jnp.where(row + r0 == 0, 1.0, mult)
        u = y * gi * mult
        a = a.reshape(groups, SUBLANES, w)
        u = u.reshape(groups, SUBLANES, w)
        for shift in (1, 2, 4):
            keep = sub >= shift
            a_sh = pltpu.roll(a, shift, axis=1)
            u_sh = pltpu.roll(u, shift, axis=1)
            u = jnp.where(keep, u + a * u_sh, u)
            a = jnp.where(keep, a * a_sh, a)
        a = a.reshape(chunk, w)
        u = u.reshape(chunk, w)
        hs = []
        h_last = h_prev
        for gidx in range(chunk // SUBLANES):
            sl = slice(gidx * SUBLANES, (gidx + 1) * SUBLANES)
            hg = u[sl] + a[sl] * h_last
            hs.append(hg)
            h_last = hg[SUBLANES - 1:SUBLANES, :]
        h = jnp.concatenate(hs, axis=0)
        gt = gate_ref[0, rows, :].astype(F32)
        half_gt = 0.5 * gt
        gelu = half_gt + half_gt * jnp.tanh(gt * (gelu_c1 + gelu_c3 * (gt * gt)))
        o_ref[0, rows, :] = (gelu * h).astype(o_ref.dtype)
        return xb[chunk - CONV_HALO:, :], h_last

    init = (jnp.zeros((CONV_HALO, w), BF16), jnp.zeros((1, w), F32))
    lax.fori_loop(0, n_chunks, body, init)


def _rglru(proj, conv_w, conv_b, wa, ba, wx, bx, lam, *, chunk_target=256):
    b, s, w2 = proj.shape
    lru = w2 // 2
    w = RG_BLOCK_WIDTH
    nb = lru // w
    chunk = _tile(s, chunk_target, 2 * SUBLANES)
    wax = (0.5 * jnp.concatenate([wa, wx], axis=-1)).astype(BF16)
    bax = 0.5 * jnp.concatenate([ba, bx], axis=-1).reshape(nb, 1, 2 * w)
    t_idx = np.arange(chunk)
    shift = np.zeros((CONV_WIDTH - 1, chunk, chunk), np.float32)
    for tap in range(1, CONV_WIDTH):
        shift[tap - 1, t_idx[tap:], t_idx[tap:] - tap] = 1.0
    shift = jnp.asarray(shift, BF16)
    kern = functools.partial(_rglru_kernel, chunk=chunk, n_chunks=s // chunk)
    return pl.pallas_call(
        kern,
        grid=(b, nb),
        in_specs=[
            pl.BlockSpec((1, s, w), lambda bi, ni: (bi, 0, ni)),
            pl.BlockSpec((1, s, w), lambda bi, ni: (bi, 0, nb + ni)),
            pl.BlockSpec((CONV_WIDTH, w), lambda bi, ni: (0, ni)),
            pl.BlockSpec((1, w), lambda bi, ni: (0, ni)),
            pl.BlockSpec((1, w, 2 * w), lambda bi, ni: (ni, 0, 0)),
            pl.BlockSpec((1, 1, 2 * w), lambda bi, ni: (ni, 0, 0)),
            pl.BlockSpec((1, w), lambda bi, ni: (0, ni)),
            pl.BlockSpec(shift.shape, lambda bi, ni: (0, 0, 0)),
        ],
        out_specs=pl.BlockSpec((1, s, w), lambda bi, ni: (bi, 0, ni)),
        out_shape=jax.ShapeDtypeStruct((b, s, lru), BF16),
        compiler_params=_params("arbitrary", "arbitrary"),
        name="rglru",
    )(proj, proj, conv_w, conv_b.reshape(1, lru), wax, bax, lam.reshape(1, lru), shift)


def kernel(x, p, mix_pre_g, mix_post_g, ffn_pre_g, ffn_post_g, ple_norm_g, w_in_even, w_out_even, hg_lb_logits, hg_norm_g, w_in_odd, conv_w, conv_b, rg_wa, rg_ba, rg_wx, rg_bx, rg_lambda, w_out_odd, w_gate_up, w_down, w_ple_up, w_ple_gate):
    b, s, d = x.shape
    t = b * s
    depth = p.shape[0]
    h = x.reshape(t, d)
    for i in range(depth):
        j = i // 2
        if i % 2 == 0:
            n_heads = w_out_even.shape[1] // (2 * HEAD_DIM)
            proj = _norm_matmul(h, mix_pre_g[i], w_in_even, j, BF16, tn_target=1024)
            proj = proj.reshape(b, s, -1)
            a_out = _sb_attention(proj, n_heads)
            b_out = _hgrn2(proj, hg_lb_logits, hg_norm_g[j], n_heads, 3 * n_heads, j)
            mixed = jnp.concatenate([a_out, b_out], axis=-1).reshape(t, -1)
            w_out = w_out_even[j].astype(BF16)
        else:
            proj = _norm_matmul(h, mix_pre_g[i], w_in_odd, j, BF16)
            mixed = _rglru(proj.reshape(b, s, -1), conv_w[j], conv_b[j], rg_wa[j], rg_ba[j],
                           rg_wx[j], rg_bx[j], rg_lambda[j]).reshape(t, -1)
            w_out = w_out_odd[j].astype(BF16)
        h = _matmul_norm_residual(mixed, w_out, mix_post_g[i], h, tk_target=2048)
        act = _norm_swiglu(h, ffn_pre_g[i], w_gate_up, i)
        h = _matmul_norm_residual(act, w_down[i].astype(BF16), ffn_post_g[i], h)
        h = _per_layer_embedding(h, p.reshape(depth, t, -1), i, w_ple_up[i].astype(BF16),
                                 w_ple_gate[i].astype(BF16), ple_norm_g[i])
    return h.reshape(b, s, d)
```
